```python
import math
import jax, jax.numpy as jnp
from jax import lax
import numpy as np

D_MODEL = 1024
BATCH = 4
SEQ = 8192
DEPTH = 1
DEC_BATCH = 128
DEC_SEQ = 4
PAST_LEN = 16384
PAGE_SIZE = 128

F32 = jnp.float32
EPS = 1e-6
N_META = 16
HG_HEADS = 4
HG_DK = 128
HG_DV = 128
HG_WIDTH = HG_HEADS * HG_DK
HG_CHUNK = 64
AT_HEADS = 8
AT_KV_HEADS = 2
AT_HEAD_DIM = 64
AT_GROUP = AT_HEADS // AT_KV_HEADS
AT_WIDTH = AT_HEADS * AT_HEAD_DIM
AT_KV_WIDTH = AT_KV_HEADS * AT_HEAD_DIM
AT_SCALE = AT_HEAD_DIM ** -0.5
WINDOW = 128
BLOCK = 128
PEER_HEADS = 8
PEER_NKEYS = 128
PEER_EXPERTS = PEER_NKEYS * PEER_NKEYS
PEER_HALF = 128
PEER_TOPK = 16
PEER_BLOCK = 256
SPLIT_SIZES = (HG_WIDTH, HG_WIDTH, HG_WIDTH, HG_WIDTH, AT_WIDTH, AT_KV_WIDTH, AT_KV_WIDTH, D_MODEL, D_MODEL)
IN_WIDTH = sum(SPLIT_SIZES)

kernel_name = "hgrn2_swa_sink_peer_hybrid_step"


def rmsnorm(x, gain):
    xf = x.astype(F32)
    y = xf * lax.rsqrt(jnp.mean(xf * xf, axis=-1, keepdims=True) + EPS)
    return (y * gain.astype(F32)).astype(x.dtype)


def alibi_slopes():
    s = 2.0 ** (-8.0 * np.arange(1, AT_HEADS + 1) / AT_HEADS)
    return jnp.asarray(s, F32).reshape(AT_KV_HEADS, AT_GROUP)


def split_proj(xn, w_in):
    p = xn @ w_in
    offs = np.cumsum(SPLIT_SIZES)[:-1].tolist()
    return jnp.split(p, offs, axis=-1)


def hgrn_inputs(hq, hf, hi, lb):
    shp = hq.shape[:-1] + (HG_HEADS, HG_DK)
    q = jax.nn.silu(hq.astype(F32)).reshape(shp)
    f = lb + (1.0 - lb) * jax.nn.sigmoid(hf.astype(F32))
    k = (1.0 - f).reshape(shp)
    logf = jnp.log(f).reshape(shp)
    v = hi.astype(F32).reshape(hq.shape[:-1] + (HG_HEADS, HG_DV))
    return q, k, v, logf


def hgrn_chunked(q, k, v, logf, s0, chunk):
    b, t = q.shape[:2]
    n = t // chunk

    def to_chunks(a):
        return jnp.moveaxis(a.reshape((b, n, chunk) + a.shape[2:]), 1, 0)

    causal = jnp.tril(jnp.ones((chunk, chunk), bool))[None, :, :, None, None]

    def step(s, inp):
        qc, kc, vc, lc = inp
        cum = jnp.cumsum(lc, axis=1)
        o_inter = jnp.einsum('bchk,bhkv->bchv', qc * jnp.exp(cum), s)
        diff = cum[:, :, None] - cum[:, None, :]
        decay = jnp.exp(jnp.where(causal, diff, -jnp.inf))
        attn = jnp.einsum('bthk,btshk,bshk->bhts', qc, decay, kc)
        o_intra = jnp.einsum('bhts,bshv->bthv', attn, vc)
        last = cum[:, -1]
        kdec = kc * jnp.exp(last[:, None] - cum)
        s_new = jnp.exp(last)[..., None] * s + jnp.einsum('bshk,bshv->bhkv', kdec, vc)
        return s_new, o_inter + o_intra

    s_fin, o = lax.scan(step, s0, (to_chunks(q), to_chunks(k), to_chunks(v), to_chunks(logf)))
    o = jnp.moveaxis(o, 0, 1).reshape((b, t) + o.shape[3:])
    return o, s_fin


def hgrn_out(o, g, gain):
    o = o * lax.rsqrt(jnp.mean(o * o, axis=-1, keepdims=True) + EPS) * gain.astype(F32)
    return (o.reshape(g.shape) * jax.nn.silu(g.astype(F32))).astype(g.dtype)


def sink_softmax(logits, sink):
    s = jnp.broadcast_to(sink[:, :, None, None], logits.shape[:-1] + (1,))
    w = jax.nn.softmax(jnp.concatenate([logits, s], axis=-1), axis=-1)
    return w[..., :-1]


def swa_prompt(aq, ak, av, sinks, slopes):
    b, t = aq.shape[:2]
    n_real = t - N_META
    nb = n_real // BLOCK
    q = aq.astype(F32).reshape(b, t, AT_KV_HEADS, AT_GROUP, AT_HEAD_DIM) * AT_SCALE
    k = ak.astype(F32).reshape(b, t, AT_KV_HEADS, AT_HEAD_DIM)
    v = av.astype(F32).reshape(b, t, AT_KV_HEADS, AT_HEAD_DIM)
    qm, km, vm = q[:, :N_META], k[:, :N_META], v[:, :N_META]
    sink = sinks.astype(F32).reshape(AT_KV_HEADS, AT_GROUP)
    slope = slopes[:, :, None, None]
    mi = jnp.arange(N_META)
    lm = jnp.einsum('bqkgd,bskd->bkgqs', qm, km)
    lm = jnp.where(mi[:, None] >= mi[None, :], lm, -jnp.inf)
    om = jnp.einsum('bkgqs,bskd->bqkgd', sink_softmax(lm, sink), vm)
    qb = q[:, N_META:].reshape(b, nb, BLOCK, AT_KV_HEADS, AT_GROUP, AT_HEAD_DIM)
    kb = k[:, N_META:].reshape(b, nb, BLOCK, AT_KV_HEADS, AT_HEAD_DIM)
    vb = v[:, N_META:].reshape(b, nb, BLOCK, AT_KV_HEADS, AT_HEAD_DIM)
    pad = ((0, 0), (1, 0), (0, 0), (0, 0), (0, 0))
    kband = jnp.concatenate([jnp.pad(kb, pad)[:, :-1], kb], axis=2)
    vband = jnp.concatenate([jnp.pad(vb, pad)[:, :-1], vb], axis=2)
    qi = jnp.arange(BLOCK)
    sj = jnp.arange(2 * BLOCK)
    dist = qi[:, None] + BLOCK - sj[None, :]
    kidx = jnp.arange(nb)[:, None] * BLOCK - BLOCK + sj[None, :]
    mask = ((dist >= 0) & (dist <= WINDOW))[None] & (kidx >= 0)[:, None, :]
    lband = jnp.einsum('bnqkgd,bnskd->bnkgqs', qb, kband) - slope * dist.astype(F32)
    lband = jnp.where(mask[None, :, None, None], lband, -jnp.inf)
    lmeta = jnp.einsum('bnqkgd,bskd->bnkgqs', qb, km)
    w = sink_softmax(jnp.concatenate([lmeta, lband], axis=-1), sink)
    orl = (jnp.einsum('bnkgqs,bskd->bnqkgd', w[..., :N_META], vm)
           + jnp.einsum('bnkgqs,bnskd->bnqkgd', w[..., N_META:], vband))
    out = jnp.concatenate([om.reshape(b, N_META, AT_WIDTH), orl.reshape(b, n_real, AT_WIDTH)], axis=1)
    return out.astype(aq.dtype)


def swa_sample(aq, ak, av, win_k, win_v, meta_k, meta_v, sinks, slopes):
    b, s = aq.shape[:2]
    q = aq.astype(F32).reshape(b, s, AT_KV_HEADS, AT_GROUP, AT_HEAD_DIM) * AT_SCALE
    k = ak.astype(F32).reshape(b, s, AT_KV_HEADS, AT_HEAD_DIM)
    v = av.astype(F32).reshape(b, s, AT_KV_HEADS, AT_HEAD_DIM)
    sink = sinks.astype(F32).reshape(AT_KV_HEADS, AT_GROUP)
    slope = slopes[:, :, None, None]
    kw = jnp.concatenate([win_k.astype(F32), k], axis=1)
    vw = jnp.concatenate([win_v.astype(F32), v], axis=1)
    qpos = PAST_LEN + jnp.arange(s)
    kpos = PAST_LEN - WINDOW + jnp.arange(WINDOW + s)
    dist = qpos[:, None] - kpos[None, :]
    mask = (dist >= 0) & (dist <= WINDOW) & (kpos >= N_META)[None, :]
    lw = jnp.einsum('bqkgd,bskd->bkgqs', q, kw) - slope * dist.astype(F32)
    lw = jnp.where(mask, lw, -jnp.inf)
    lm = jnp.einsum('bqkgd,bskd->bkgqs', q, meta_k.astype(F32))
    w = sink_softmax(jnp.concatenate([lm, lw], axis=-1), sink)
    o = (jnp.einsum('bkgqs,bskd->bqkgd', w[..., :N_META], meta_v.astype(F32))
         + jnp.einsum('bkgqs,bskd->bqkgd', w[..., N_META:], vw))
    return o.reshape(b, s, AT_WIDTH).astype(aq.dtype)


def peer_ffn(xn, wq, subkeys, u_tab, v_tab):
    lead = xn.shape[:-1]
    xf = xn.reshape(-1, D_MODEL)
    n = xf.shape[0]
    xf = jnp.pad(xf, ((0, (-n) % PEER_BLOCK), (0, 0)))
    xb = xf.reshape(-1, PEER_BLOCK, D_MODEL)
    hk = PEER_HEADS * PEER_TOPK

    def block(xt):
        q = (xt @ wq).reshape(PEER_BLOCK, PEER_HEADS, 2, PEER_HALF)
        s = jnp.einsum('thcd,hcnd->thcn', q, subkeys).astype(F32)
        s1, i1 = lax.top_k(s[:, :, 0], PEER_TOPK)
        s2, i2 = lax.top_k(s[:, :, 1], PEER_TOPK)
        cand = (s1[..., :, None] + s2[..., None, :]).reshape(PEER_BLOCK, PEER_HEADS, PEER_TOPK * PEER_TOPK)
        cidx = (i1[..., :, None] * PEER_NKEYS + i2[..., None, :]).reshape(PEER_BLOCK, PEER_HEADS, PEER_TOPK * PEER_TOPK)
        top_s, pos = lax.top_k(cand, PEER_TOPK)
        eidx = jnp.take_along_axis(cidx, pos, axis=-1).reshape(PEER_BLOCK, hk)
        gate = jax.nn.softmax(top_s, axis=-1).reshape(PEER_BLOCK, hk)
        ug = jnp.take(u_tab, eidx, axis=0)
        vg = jnp.take(v_tab, eidx, axis=0)
        hid = jax.nn.gelu(jnp.einsum('ted,td->te', ug, xt).astype(F32), approximate=False)
        w = (gate * hid).astype(xt.dtype)
        return jnp.einsum('te,ted->td', w, vg)

    y = lax.map(block, xb).reshape(-1, D_MODEL)[:n]
    return y.reshape(lead + (D_MODEL,))


def merge_branches(oa, ob, ga, gb, w_pa, w_pb, w_o):
    m = jax.nn.sigmoid(ga) * (oa @ w_pa) + jax.nn.sigmoid(gb) * (ob @ w_pb)
    return m @ w_o


def setup_inputs(seed: int = 0) -> dict:
    key = jax.random.key(seed)
    ks = jax.random.split(key, 24)

    def nrm(k, shape, s):
        return jax.random.normal(k, shape, F32) * s

    def gain(k, shape):
        return 1.0 + 0.01 * jax.random.normal(k, shape, F32)

    return {
        "x_prompt": nrm(ks[0], (BATCH, SEQ, D_MODEL), 1.0),
        "x_sample": nrm(ks[1], (DEC_BATCH, DEC_SEQ, D_MODEL), 1.0),
        "cache_win_k": nrm(ks[2], (DEPTH, DEC_BATCH, WINDOW, AT_KV_HEADS, AT_HEAD_DIM), 1.0),
        "cache_win_v": nrm(ks[3], (DEPTH, DEC_BATCH, WINDOW, AT_KV_HEADS, AT_HEAD_DIM), 1.0),
        "cache_meta_k": nrm(ks[4], (DEPTH, DEC_BATCH, N_META, AT_KV_HEADS, AT_HEAD_DIM), 1.0),
        "cache_meta_v": nrm(ks[5], (DEPTH, DEC_BATCH, N_META, AT_KV_HEADS, AT_HEAD_DIM), 1.0),
        "state_hgrn": nrm(ks[6], (DEPTH, DEC_BATCH, HG_HEADS, HG_DK, HG_DV), 0.5),
        "meta_tokens": nrm(ks[7], (N_META, D_MODEL), 1.0),
        "norm_mix": gain(ks[8], (DEPTH, D_MODEL)),
        "w_in": nrm(ks[9], (DEPTH, D_MODEL, IN_WIDTH), D_MODEL ** -0.5),
        "hg_lb": nrm(ks[10], (DEPTH + 1, HG_WIDTH), 0.5),
        "hg_norm": gain(ks[11], (DEPTH, HG_DV)),
        "attn_sinks": nrm(ks[12], (DEPTH, AT_HEADS), 0.5),
        "w_pa": nrm(ks[13], (DEPTH, HG_WIDTH, D_MODEL), HG_WIDTH ** -0.5),
        "w_pb": nrm(ks[14], (DEPTH, AT_WIDTH, D_MODEL), AT_WIDTH ** -0.5),
        "w_o": nrm(ks[15], (DEPTH, D_MODEL, D_MODEL), D_MODEL ** -0.5),
        "norm_ffn": gain(ks[16], (DEPTH, D_MODEL)),
        "peer_wq": nrm(ks[17], (DEPTH, D_MODEL, PEER_HEADS * 2 * PEER_HALF), D_MODEL ** -0.5),
        "peer_subkeys": nrm(ks[18], (DEPTH, PEER_HEADS, 2, PEER_NKEYS, PEER_HALF), PEER_HALF ** -0.5),
        "peer_u": nrm(ks[19], (DEPTH, PEER_EXPERTS, D_MODEL), D_MODEL ** -0.5),
        "peer_v": nrm(ks[20], (DEPTH, PEER_EXPERTS, D_MODEL), PEER_HEADS ** -0.5),
        "norm_final": gain(ks[21], (D_MODEL,)),
    }


def reference(x_prompt, x_sample, cache_win_k, cache_win_v, cache_meta_k, cache_meta_v, state_hgrn,
              meta_tokens, norm_mix, w_in, hg_lb, hg_norm, attn_sinks, w_pa, w_pb, w_o, norm_ffn,
              peer_wq, peer_subkeys, peer_u, peer_v, norm_final):
    slopes = alibi_slopes()
    lb_all = jnp.cumsum(jax.nn.softmax(hg_lb.astype(F32), axis=0), axis=0)
    b = x_prompt.shape[0]
    meta = jnp.broadcast_to(meta_tokens[None].astype(x_prompt.dtype), (b, N_META, D_MODEL))
    hp = jnp.concatenate([meta, x_prompt], axis=1)
    hs = x_sample
    ds = hs.shape[1]
    wkp, wvp, mkp, mvp, shp, wks, wvs, shs = [], [], [], [], [], [], [], []
    for l in range(DEPTH):
        xn = rmsnorm(hp, norm_mix[l])
        hq, hf, hi, hg, aq, ak, av, ga, gb = split_proj(xn, w_in[l])
        q, k, v, logf = hgrn_inputs(hq, hf, hi, lb_all[l])
        s0 = jnp.zeros((b, HG_HEADS, HG_DK, HG_DV), F32)
        o_m, s_meta = hgrn_chunked(q[:, :N_META], k[:, :N_META], v[:, :N_META], logf[:, :N_META], s0, N_META)
        o_r, s_fin = hgrn_chunked(q[:, N_META:], k[:, N_META:], v[:, N_META:], logf[:, N_META:], s_meta, HG_CHUNK)
        oa = hgrn_out(jnp.concatenate([o_m, o_r], axis=1), hg, hg_norm[l])
        ob = swa_prompt(aq, ak, av, attn_sinks[l], slopes)
        hp = hp + merge_branches(oa, ob, ga, gb, w_pa[l], w_pb[l], w_o[l])
        hp = hp + peer_ffn(rmsnorm(hp, norm_ffn[l]), peer_wq[l], peer_subkeys[l], peer_u[l], peer_v[l])
        akr = ak.reshape(b, -1, AT_KV_HEADS, AT_HEAD_DIM)
        avr = av.reshape(b, -1, AT_KV_HEADS, AT_HEAD_DIM)
        wkp.append(akr[:, -WINDOW:])
        wvp.append(avr[:, -WINDOW:])
        mkp.append(akr[:, :N_META])
        mvp.append(avr[:, :N_META])
        shp.append(s_fin.astype(x_prompt.dtype))
        xn = rmsnorm(hs, norm_mix[l])
        hq, hf, hi, hg, aq, ak, av, ga, gb = split_proj(xn, w_in[l])
        q, k, v, logf = hgrn_inputs(hq, hf, hi, lb_all[l])
        o_s, s_new = hgrn_chunked(q, k, v, logf, state_hgrn[l].astype(F32), ds)
        oa = hgrn_out(o_s, hg, hg_norm[l])
        ob = swa_sample(aq, ak, av, cache_win_k[l], cache_win_v[l], cache_meta_k[l], cache_meta_v[l],
                        attn_sinks[l], slopes)
        hs = hs + merge_branches(oa, ob, ga, gb, w_pa[l], w_pb[l], w_o[l])
        hs = hs + peer_ffn(rmsnorm(hs, norm_ffn[l]), peer_wq[l], peer_subkeys[l], peer_u[l], peer_v[l])
        wks.append(ak.reshape(hs.shape[0], ds, AT_KV_HEADS, AT_HEAD_DIM))
        wvs.append(av.reshape(hs.shape[0], ds, AT_KV_HEADS, AT_HEAD_DIM))
        shs.append(s_new.astype(x_sample.dtype))
    y_prompt = rmsnorm(hp, norm_final)[:, N_META:]
    y_sample = rmsnorm(hs, norm_final)
    return (y_prompt, y_sample,
            jnp.stack(wkp, 0), jnp.stack(wvp, 0), jnp.stack(mkp, 0), jnp.stack(mvp, 0), jnp.stack(shp, 0),
            jnp.stack(wks, 0), jnp.stack(wvs, 0), jnp.stack(shs, 0))
```

```python
import functools
import math

import numpy as np
import jax
import jax.numpy as jnp
from jax import lax
from jax.experimental import pallas as pl
from jax.experimental.pallas import tpu as pltpu

F32 = jnp.float32
BF16 = jnp.bfloat16
EPS = 1e-6
D_MODEL = 1024
N_META = 16
HG_HEADS = 4
HG_DK = 128
HG_WIDTH = HG_HEADS * HG_DK
HG_CHUNK = 64
AT_HEADS = 8
AT_KV_HEADS = 2
AT_GROUP = AT_HEADS // AT_KV_HEADS
AT_HEAD_DIM = 64
AT_WIDTH = AT_HEADS * AT_HEAD_DIM
AT_KV_WIDTH = AT_KV_HEADS * AT_HEAD_DIM
AT_SCALE = AT_HEAD_DIM ** -0.5
WINDOW = 128
BLOCK = 128
PAST_LEN = 16384
PEER_HEADS = 8
PEER_NKEYS = 128
PEER_HALF = 128
PEER_TOPK = 16
PEER_PICKS = PEER_HEADS * PEER_TOPK
SUBLANES = 8
LANES = 128
ROW_WORDS = D_MODEL // 2
ROW_SUBLANES = ROW_WORDS // LANES
VMEM_LIMIT = 56 * 1024 * 1024

NT_DIMS = (((1,), (1,)), ((), ()))
TN_DIMS = (((0,), (0,)), ((), ()))


def _cparams(*sem):
    return pltpu.CompilerParams(dimension_semantics=sem, vmem_limit_bytes=VMEM_LIMIT)


def _tile(n, pref):
    return pref if n % pref == 0 else n


def _sigmoid(x):
    return 1.0 / (1.0 + jnp.exp(-x))


def _proj_kernel(x_ref, gain_ref, lb_ref, w_ref, q_ref, k_ref, lf_ref, v_ref, sg_ref,
                 aq_ref, ak_ref, av_ref, sga_ref, sgb_ref):
    x = x_ref[...]
    xn = x * lax.rsqrt(jnp.mean(x * x, axis=-1, keepdims=True) + EPS) * gain_ref[...]
    xb = xn.astype(BF16)

    def proj(a, b):
        return jnp.dot(xb, w_ref[:, a:b], preferred_element_type=F32)

    o = 0
    hq = proj(o, o + HG_WIDTH); o += HG_WIDTH
    q_ref[...] = hq * _sigmoid(hq)
    lb = lb_ref[...]
    f = lb + (1.0 - lb) * _sigmoid(proj(o, o + HG_WIDTH)); o += HG_WIDTH
    k_ref[...] = 1.0 - f
    lf_ref[...] = jnp.log(f)
    v_ref[...] = proj(o, o + HG_WIDTH); o += HG_WIDTH
    hg = proj(o, o + HG_WIDTH); o += HG_WIDTH
    sg_ref[...] = hg * _sigmoid(hg)
    aq_ref[...] = proj(o, o + AT_WIDTH) * AT_SCALE; o += AT_WIDTH
    ak_ref[...] = proj(o, o + AT_KV_WIDTH); o += AT_KV_WIDTH
    av_ref[...] = proj(o, o + AT_KV_WIDTH); o += AT_KV_WIDTH
    sga_ref[...] = _sigmoid(proj(o, o + D_MODEL)); o += D_MODEL
    sgb_ref[...] = _sigmoid(proj(o, o + D_MODEL))


def _proj(x, gain, lb, w_bf16, tm):
    n = x.shape[0]
    widths = (HG_WIDTH,) * 5 + (AT_WIDTH, AT_KV_WIDTH, AT_KV_WIDTH, D_MODEL, D_MODEL)
    row = lambda w: pl.BlockSpec((tm, w), lambda i: (i, 0))
    full = lambda a: pl.BlockSpec(a.shape, lambda i: (0,) * a.ndim)
    return pl.pallas_call(
        _proj_kernel,
        grid=(n // tm,),
        in_specs=[row(D_MODEL), full(gain), full(lb), full(w_bf16)],
        out_specs=[row(w) for w in widths],
        out_shape=[jax.ShapeDtypeStruct((n, w), F32) for w in widths],
        compiler_params=_cparams("parallel"),
        name="proj",
    )(x, gain, lb, w_bf16)


def _hgrn_consts(c):
    levels = int(round(math.log2(c)))
    t = np.arange(c)
    le = (t[None, :] <= t[:, None]).astype(np.float32)
    mats = [le]
    masks = []
    for l in range(levels):
        bs, half = 2 << l, 1 << l
        base = (t // bs) * bs
        bnd = base + half - 1
        mats.append(le - (t[None, :] <= bnd[:, None]).astype(np.float32))
        right = (t % bs) >= half
        masks.append((base[:, None] == base[None, :]) & right[:, None] & (~right)[None, :])
    masks.append(np.eye(c, dtype=bool))
    a = np.concatenate(mats, 0)
    a2 = np.concatenate([a, a], 1)
    return jnp.asarray(a2, BF16), jnp.asarray(np.stack(masks).astype(np.float32))


def _hgrn_kernel(a_ref, m_ref, q_ref, k_ref, lf_ref, v_ref, sg_ref, gain_ref, s0_ref, o_ref, sfin_ref, st_ref,
                 *, c, nchunk):
    levels = int(round(math.log2(c)))
    i = pl.program_id(1)

    @pl.when(i == 0)
    def _():
        for h in range(HG_HEADS):
            st_ref[h] = s0_ref[0, h].T

    gain = gain_ref[...]

    def chunk(ci, carry):
        r0 = pl.multiple_of(ci * c, c)
        for h in range(HG_HEADS):
            sl = (pl.ds(r0, c), slice(h * HG_DK, (h + 1) * HG_DK))
            q = q_ref[sl]
            k = k_ref[sl]
            v = v_ref[sl]
            lf = lf_ref[sl]
            st = st_ref[h]
            hi = lf.astype(BF16)
            lo = (lf - hi.astype(F32)).astype(BF16)
            r = jnp.dot(a_ref[...], jnp.concatenate([hi, lo], axis=0), preferred_element_type=F32)
            cum = r[0:c]
            qb = q.astype(BF16)
            kb = k.astype(BF16)
            vb = v.astype(BF16)
            o = lax.dot_general((q * jnp.exp(cum)).astype(BF16), st.astype(BF16), NT_DIMS,
                                preferred_element_type=F32)
            att = m_ref[levels] * lax.dot_general(qb, kb, NT_DIMS, preferred_element_type=F32)
            for l in range(levels):
                e = r[(l + 1) * c:(l + 2) * c]
                aq = (q * jnp.exp(jnp.minimum(e, 0.0))).astype(BF16)
                ak = (k * jnp.exp(jnp.minimum(-e, 0.0))).astype(BF16)
                att = att + m_ref[l] * lax.dot_general(aq, ak, NT_DIMS, preferred_element_type=F32)
            o = o + jnp.dot(att.astype(BF16), vb, preferred_element_type=F32)
            last = cum[c - 1:c]
            kdec = (k * jnp.exp(last - cum)).astype(BF16)
            st_ref[h] = st * jnp.exp(last) + lax.dot_general(vb, kdec, TN_DIMS, preferred_element_type=F32)
            on = o * lax.rsqrt(jnp.mean(o * o, axis=-1, keepdims=True) + EPS) * gain
            o_ref[sl] = on * sg_ref[sl]
        return carry

    lax.fori_loop(0, nchunk, chunk, 0)

    @pl.when(i == pl.num_programs(1) - 1)
    def _():
        for h in range(HG_HEADS):
            sfin_ref[0, h] = st_ref[h].T


def _hgrn(q, k, lf, v, sg, gain, s0, batch, c, ct):
    n = q.shape[0]
    t = n // batch
    steps = t // ct
    a2, masks = _hgrn_consts(c)
    row = pl.BlockSpec((ct, HG_WIDTH), lambda b, i: (b * steps + i, 0))
    full = lambda a: pl.BlockSpec(a.shape, lambda b, i: (0,) * a.ndim)
    shared = s0.shape[0] == 1
    sspec = pl.BlockSpec((1, HG_HEADS, HG_DK, HG_DK), (lambda b, i: (0, 0, 0, 0)) if shared else (lambda b, i: (b, 0, 0, 0)))
    return pl.pallas_call(
        functools.partial(_hgrn_kernel, c=c, nchunk=ct // c),
        grid=(batch, steps),
        in_specs=[full(a2), full(masks), row, row, row, row, row, full(gain), sspec],
        out_specs=[row, pl.BlockSpec((1, HG_HEADS, HG_DK, HG_DK), lambda b, i: (b, 0, 0, 0))],
        out_shape=[jax.ShapeDtypeStruct((n, HG_WIDTH), F32),
                   jax.ShapeDtypeStruct((batch, HG_HEADS, HG_DK, HG_DK), F32)],
        scratch_shapes=[pltpu.VMEM((HG_HEADS, HG_DK, HG_DK), F32)],
        compiler_params=_cparams("parallel", "arbitrary"),
        name="hgrn",
    )(a2, masks, q, k, lf, v, sg, gain, s0)


def _swa_prompt_kernel(sink_ref, q_ref, kc_ref, kp_ref, vc_ref, vp_ref, km_ref, vm_ref, o_ref):
    n = pl.program_id(1)
    qi = lax.broadcasted_iota(jnp.int32, (BLOCK, 2 * BLOCK), 0)
    sj = lax.broadcasted_iota(jnp.int32, (BLOCK, 2 * BLOCK), 1)
    dist = qi + BLOCK - sj
    valid = (dist >= 0) & (dist <= WINDOW) & ((sj >= BLOCK) | (n > 0))
    distf = dist.astype(F32)
    for kv in range(AT_KV_HEADS):
        ks = slice(kv * AT_HEAD_DIM, (kv + 1) * AT_HEAD_DIM)
        kband = jnp.concatenate([kp_ref[:, ks], kc_ref[:, ks]], axis=0).astype(BF16)
        vband = jnp.concatenate([vp_ref[:, ks], vc_ref[:, ks]], axis=0).astype(BF16)
        kmeta = km_ref[:, ks].astype(BF16)
        vmeta = vm_ref[:, ks].astype(BF16)
        for g in range(AT_GROUP):
            h = kv * AT_GROUP + g
            hs = slice(h * AT_HEAD_DIM, (h + 1) * AT_HEAD_DIM)
            slope = 2.0 ** (-8.0 * (h + 1) / AT_HEADS)
            qh = q_ref[:, hs].astype(BF16)
            lband = lax.dot_general(qh, kband, NT_DIMS, preferred_element_type=F32) - slope * distf
            lband = jnp.where(valid, lband, -jnp.inf)
            lmeta = lax.dot_general(qh, kmeta, NT_DIMS, preferred_element_type=F32)
            sink = sink_ref[h]
            m = jnp.maximum(jnp.maximum(jnp.max(lband, axis=-1, keepdims=True),
                                        jnp.max(lmeta, axis=-1, keepdims=True)), sink)
            eb = jnp.exp(lband - m)
            em = jnp.exp(lmeta - m)
            den = (jnp.sum(eb, axis=-1, keepdims=True) + jnp.sum(em, axis=-1, keepdims=True)
                   + jnp.exp(sink - m))
            o = (jnp.dot(eb.astype(BF16), vband, preferred_element_type=F32)
                 + jnp.dot(em.astype(BF16), vmeta, preferred_element_type=F32))
            o_ref[:, hs] = o / den


def _swa_prompt(sinks, aq, ak, av, km, vm, batch):
    n = aq.shape[0]
    nb = n // batch // BLOCK
    cur = lambda w: pl.BlockSpec((BLOCK, w), lambda b, i: (b * nb + i, 0))
    prev = lambda w: pl.BlockSpec((BLOCK, w), lambda b, i: (b * nb + jnp.maximum(i - 1, 0), 0))
    meta = pl.BlockSpec((N_META, AT_KV_WIDTH), lambda b, i: (0, 0))
    return pl.pallas_call(
        _swa_prompt_kernel,
        grid=(batch, nb),
        in_specs=[pl.BlockSpec(memory_space=pltpu.SMEM), cur(AT_WIDTH), cur(AT_KV_WIDTH), prev(AT_KV_WIDTH),
                  cur(AT_KV_WIDTH), prev(AT_KV_WIDTH), meta, meta],
        out_specs=cur(AT_WIDTH),
        out_shape=jax.ShapeDtypeStruct((n, AT_WIDTH), F32),
        compiler_params=_cparams("parallel", "parallel"),
        name="swa_prompt",
    )(sinks, aq, ak, ak, av, av, km, vm)


def _swa_sample_kernel(sink_ref, q_ref, kn_ref, vn_ref, kw_ref, vw_ref, km_ref, vm_ref, o_ref, *, bg, ds):
    rows = AT_GROUP * ds
    nk = WINDOW + ds
    ri = lax.broadcasted_iota(jnp.int32, (rows, nk), 0)
    sj = lax.broadcasted_iota(jnp.int32, (rows, nk), 1)
    qpos = PAST_LEN + ri % ds
    kpos = PAST_LEN - WINDOW + sj
    dist = qpos - kpos
    valid = (dist >= 0) & (dist <= WINDOW) & (kpos >= N_META)
    distf = dist.astype(F32)
    gi = lax.broadcasted_iota(jnp.int32, (rows, 1), 0) // ds
    for kv in range(AT_KV_HEADS):
        ks = slice(kv * AT_HEAD_DIM, (kv + 1) * AT_HEAD_DIM)
        slope = jnp.exp2(-(gi + (kv * AT_GROUP + 1)).astype(F32) * (8.0 / AT_HEADS))
        sink = jnp.zeros((rows, 1), F32)
        for g in range(AT_GROUP):
            sink = jnp.where(gi == g, sink_ref[kv * AT_GROUP + g], sink)
        for b in range(bg):
            kall = jnp.concatenate([kw_ref[b, :, ks], kn_ref[b, :, ks]], axis=0).astype(BF16)
            vall = jnp.concatenate([vw_ref[b, :, ks], vn_ref[b, :, ks]], axis=0).astype(BF16)
            kmeta = km_ref[b, :, ks].astype(BF16)
            vmeta = vm_ref[b, :, ks].astype(BF16)
            qs = jnp.concatenate(
                [q_ref[b, :, (kv * AT_GROUP + g) * AT_HEAD_DIM:(kv * AT_GROUP + g + 1) * AT_HEAD_DIM]
                 for g in range(AT_GROUP)], axis=0).astype(BF16)
            lw = lax.dot_general(qs, kall, NT_DIMS, preferred_element_type=F32) - slope * distf
            lw = jnp.where(valid, lw, -jnp.inf)
            lm = lax.dot_general(qs, kmeta, NT_DIMS, preferred_element_type=F32)
            m = jnp.maximum(jnp.maximum(jnp.max(lw, axis=-1, keepdims=True),
                                        jnp.max(lm, axis=-1, keepdims=True)), sink)
            ew = jnp.exp(lw - m)
            em = jnp.exp(lm - m)
            den = (jnp.sum(ew, axis=-1, keepdims=True) + jnp.sum(em, axis=-1, keepdims=True)
                   + jnp.exp(sink - m))
            o = (jnp.dot(ew.astype(BF16), vall, preferred_element_type=F32)
                 + jnp.dot(em.astype(BF16), vmeta, preferred_element_type=F32)) / den
            for g in range(AT_GROUP):
                h = kv * AT_GROUP + g
                o_ref[b, :, h * AT_HEAD_DIM:(h + 1) * AT_HEAD_DIM] = o[g * ds:(g + 1) * ds]


def _swa_sample(sinks, aq, ak, av, win_k, win_v, meta_k, meta_v, bg):
    b, ds = aq.shape[:2]
    spec = lambda a: pl.BlockSpec((bg,) + a.shape[1:], lambda i: (i, 0, 0))
    args = (aq, ak, av, win_k, win_v, meta_k, meta_v)
    return pl.pallas_call(
        functools.partial(_swa_sample_kernel, bg=bg, ds=ds),
        grid=(b // bg,),
        in_specs=[pl.BlockSpec(memory_space=pltpu.SMEM)] + [spec(a) for a in args],
        out_specs=spec(aq),
        out_shape=jax.ShapeDtypeStruct(aq.shape, F32),
        compiler_params=_cparams("parallel"),
        name="swa_sample",
    )(sinks, *args)


def _merge_kernel(x_ref, oa_ref, ob_ref, sga_ref, sgb_ref, wpa_ref, wpb_ref, wo_ref, gain_ref, wq_ref, sk_ref,
                  h2_ref, xn_ref, sc_ref):
    pa = jnp.dot(oa_ref[...].astype(BF16), wpa_ref[...], preferred_element_type=F32)
    pb = jnp.dot(ob_ref[...].astype(BF16), wpb_ref[...], preferred_element_type=F32)
    m = sga_ref[...] * pa + sgb_ref[...] * pb
    h2 = x_ref[...] + jnp.dot(m.astype(BF16), wo_ref[...], preferred_element_type=F32)
    h2_ref[...] = h2
    xn = h2 * lax.rsqrt(jnp.mean(h2 * h2, axis=-1, keepdims=True) + EPS) * gain_ref[...]
    xn_ref[...] = xn
    qp = jnp.dot(xn.astype(BF16), wq_ref[...], preferred_element_type=F32).astype(BF16)
    for hc in range(2 * PEER_HEADS):
        sc_ref[hc] = lax.dot_general(sk_ref[hc], qp[:, hc * PEER_HALF:(hc + 1) * PEER_HALF], NT_DIMS,
                                     preferred_element_type=F32)


def _merge(x, oa, ob, sga, sgb, wpa, wpb, wo, gain, wq, sk, tm):
    n = x.shape[0]
    row = lambda w: pl.BlockSpec((tm, w), lambda i: (i, 0))
    full = lambda a: pl.BlockSpec(a.shape, lambda i: (0,) * a.ndim)
    return pl.pallas_call(
        _merge_kernel,
        grid=(n // tm,),
        in_specs=[row(D_MODEL), row(HG_WIDTH), row(AT_WIDTH), row(D_MODEL), row(D_MODEL),
                  full(wpa), full(wpb), full(wo), full(gain), full(wq), full(sk)],
        out_specs=[row(D_MODEL), row(D_MODEL),
                   pl.BlockSpec((2 * PEER_HEADS, PEER_NKEYS, tm), lambda i: (0, 0, i))],
        out_shape=[jax.ShapeDtypeStruct((n, D_MODEL), F32), jax.ShapeDtypeStruct((n, D_MODEL), F32),
                   jax.ShapeDtypeStruct((2 * PEER_HEADS, PEER_NKEYS, n), F32)],
        compiler_params=_cparams("parallel"),
        name="merge",
    )(x, oa, ob, sga, sgb, wpa, wpb, wo, gain, wq, sk)


def _top16(s):
    rows = s.shape[0]
    io = lax.broadcasted_iota(jnp.int32, s.shape, 0)
    vals, idxs = [], []
    for _ in range(PEER_TOPK):
        m = jnp.max(s, axis=0, keepdims=True)
        idx = jnp.min(jnp.where(s == m, io, rows), axis=0, keepdims=True)
        vals.append(m)
        idxs.append(idx)
        s = jnp.where(io == idx, -jnp.inf, s)
    return vals, idxs


def _topk_kernel(s_ref, eidx_ref, gate_ref):
    io16 = lax.broadcasted_iota(jnp.int32, (PEER_TOPK, s_ref.shape[2]), 0)
    for h in range(PEER_HEADS):
        v1, i1 = _top16(s_ref[2 * h])
        v2, i2 = _top16(s_ref[2 * h + 1])
        v2s = jnp.concatenate(v2, axis=0)
        i1s = jnp.concatenate(i1, axis=0)
        i2s = jnp.concatenate(i2, axis=0)
        cand = jnp.concatenate([v1[a] + v2s for a in range(PEER_TOPK)], axis=0)
        ctop, cpos = _top16(cand)
        es = []
        for j in range(PEER_TOPK):
            a = cpos[j] // PEER_TOPK
            b = cpos[j] % PEER_TOPK
            e1 = jnp.sum(jnp.where(io16 == a, i1s, 0), axis=0, keepdims=True)
            e2 = jnp.sum(jnp.where(io16 == b, i2s, 0), axis=0, keepdims=True)
            es.append(e1 * PEER_NKEYS + e2)
        ts = jnp.concatenate(ctop, axis=0)
        ex = jnp.exp(ts - ctop[0])
        sl = slice(h * PEER_TOPK, (h + 1) * PEER_TOPK)
        gate_ref[sl, :] = ex / jnp.sum(ex, axis=0, keepdims=True)
        eidx_ref[sl, :] = jnp.concatenate(es, axis=0)


def _topk(scores_t, tk):
    n = scores_t.shape[2]
    return pl.pallas_call(
        _topk_kernel,
        grid=(n // tk,),
        in_specs=[pl.BlockSpec((2 * PEER_HEADS, PEER_NKEYS, tk), lambda i: (0, 0, i))],
        out_specs=[pl.BlockSpec((PEER_PICKS, tk), lambda i: (0, i))] * 2,
        out_shape=[jax.ShapeDtypeStruct((PEER_PICKS, n), jnp.int32),
                   jax.ShapeDtypeStruct((PEER_PICKS, n), F32)],
        compiler_params=_cparams("parallel"),
        name="topk",
    )(scores_t)


def _pack_table(tab):
    e = tab.shape[0]
    bits = lax.bitcast_convert_type(tab.astype(BF16), jnp.uint16).astype(jnp.uint32)
    words = bits[:, :ROW_WORDS] | (bits[:, ROW_WORDS:] << 16)
    return lax.bitcast_convert_type(words, jnp.int32).reshape(e * ROW_SUBLANES, LANES)


def _load_row(tab_ref, e):
    w = tab_ref[pl.ds(pl.multiple_of(e * ROW_SUBLANES, ROW_SUBLANES), ROW_SUBLANES), :]
    lo = pltpu.bitcast(w << 16, F32)
    hi = pltpu.bitcast(w & jnp.int32(-65536), F32)
    return lo, hi


def _peer_a_kernel(idx_ref, x_ref, gate_ref, tab_ref, w_ref, h_ref, *, tb):
    lane = lax.broadcasted_iota(jnp.int32, (PEER_PICKS, tb), 1)
    h_ref[...] = jnp.zeros_like(h_ref)

    def tok(t, carry):
        xv = x_ref[t]
        xlo = xv[0:ROW_SUBLANES]
        xhi = xv[ROW_SUBLANES:SUBLANES]
        rows = []
        for j in range(PEER_PICKS):
            lo, hi = _load_row(tab_ref, idx_ref[t * PEER_PICKS + j])
            rows.append(jnp.sum(lo * xlo + hi * xhi, axis=0, keepdims=True))
        col = jnp.sum(jnp.concatenate(rows, axis=0), axis=1, keepdims=True)
        h_ref[...] = jnp.where(lane == t, col, h_ref[...])
        return carry

    lax.fori_loop(0, tb, tok, 0)
    h = h_ref[...]
    w_ref[...] = gate_ref[...] * (0.5 * h * (1.0 + lax.erf(h * math.sqrt(0.5))))


def _peer_a(eidx_flat, xn3, gate_t, tab, tb):
    n = xn3.shape[0]
    return pl.pallas_call(
        functools.partial(_peer_a_kernel, tb=tb),
        grid=(n // tb,),
        in_specs=[pl.BlockSpec((tb * PEER_PICKS,), lambda i: (i,), memory_space=pltpu.SMEM),
                  pl.BlockSpec((tb, SUBLANES, LANES), lambda i: (i, 0, 0)),
                  pl.BlockSpec((PEER_PICKS, tb), lambda i: (0, i)),
                  pl.BlockSpec(tab.shape, lambda i: (0, 0), pipeline_mode=pl.Buffered(1))],
        out_specs=pl.BlockSpec((PEER_PICKS, tb), lambda i: (0, i)),
        out_shape=jax.ShapeDtypeStruct((PEER_PICKS, n), F32),
        scratch_shapes=[pltpu.VMEM((PEER_PICKS, tb), F32)],
        compiler_params=_cparams("arbitrary"),
        name="peer_a",
    )(eidx_flat, xn3, gate_t, tab)


def _peer_b_kernel(idx_ref, ws_ref, h2_ref, gain_ref, tab_ref, y_ref, *, tb):
    nacc = 4

    def tok(t, carry):
        acc_lo = [jnp.zeros((ROW_SUBLANES, LANES), F32) for _ in range(nacc)]
        acc_hi = [jnp.zeros((ROW_SUBLANES, LANES), F32) for _ in range(nacc)]
        for j in range(PEER_PICKS):
            lo, hi = _load_row(tab_ref, idx_ref[t * PEER_PICKS + j])
            s = ws_ref[t * PEER_PICKS + j]
            acc_lo[j % nacc] = acc_lo[j % nacc] + s * lo
            acc_hi[j % nacc] = acc_hi[j % nacc] + s * hi
        lo = (acc_lo[0] + acc_lo[1]) + (acc_lo[2] + acc_lo[3])
        hi = (acc_hi[0] + acc_hi[1]) + (acc_hi[2] + acc_hi[3])
        y = h2_ref[t] + jnp.concatenate([lo, hi], axis=0)
        ms = jnp.sum(jnp.sum(y * y, axis=1, keepdims=True), axis=0, keepdims=True) * (1.0 / D_MODEL)
        y_ref[t] = y * lax.rsqrt(ms + EPS) * gain_ref[...]
        return carry

    lax.fori_loop(0, tb, tok, 0)


def _peer_b(eidx_flat, w_flat, h23, gain3, tab, tb):
    n = h23.shape[0]
    smem = lambda: pl.BlockSpec((tb * PEER_PICKS,), lambda i: (i,), memory_space=pltpu.SMEM)
    tok3 = pl.BlockSpec((tb, SUBLANES, LANES), lambda i: (i, 0, 0))
    return pl.pallas_call(
        functools.partial(_peer_b_kernel, tb=tb),
        grid=(n // tb,),
        in_specs=[smem(), smem(), tok3, pl.BlockSpec(gain3.shape, lambda i: (0, 0)),
                  pl.BlockSpec(tab.shape, lambda i: (0, 0), pipeline_mode=pl.Buffered(1))],
        out_specs=tok3,
        out_shape=jax.ShapeDtypeStruct(h23.shape, F32),
        compiler_params=_cparams("arbitrary"),
        name="peer_b",
    )(eidx_flat, w_flat, h23, gain3, tab)


def _mix_and_ffn(x, oa, ob, sga, sgb, wts):
    n = x.shape[0]
    tm, tk, tb = _tile(n, 256), _tile(n, LANES), _tile(n, LANES)
    h2, xn, scores_t = _merge(x, oa, ob, sga, sgb, wts["wpa"], wts["wpb"], wts["wo"], wts["gain_ffn"],
                              wts["wq"], wts["sk"], tm)
    eidx_t, gate_t = _topk(scores_t, tk)
    eidx_flat = eidx_t.T.reshape(n * PEER_PICKS)
    w_t = _peer_a(eidx_flat, xn.reshape(n, SUBLANES, LANES), gate_t, wts["u"], tb)
    y3 = _peer_b(eidx_flat, w_t.T.reshape(n * PEER_PICKS), h2.reshape(n, SUBLANES, LANES), wts["gain_final"],
                 wts["v"], tb)
    return y3.reshape(n, D_MODEL)


def kernel(x_prompt, x_sample, cache_win_k, cache_win_v, cache_meta_k, cache_meta_v, state_hgrn, meta_tokens,
           norm_mix, w_in, hg_lb, hg_norm, attn_sinks, w_pa, w_pb, w_o, norm_ffn, peer_wq, peer_subkeys,
           peer_u, peer_v, norm_final):
    depth = w_in.shape[0]
    assert depth == 1, "single-layer step only"
    b, seq, _ = x_prompt.shape
    db, ds, _ = x_sample.shape
    l = 0
    lb = jnp.cumsum(jax.nn.softmax(hg_lb.astype(F32), axis=0), axis=0)[l][None, :]
    gain_mix = norm_mix[l][None, :]
    w_in_b = w_in[l].astype(BF16)
    hg_gain = hg_norm[l][None, :]
    sinks = attn_sinks[l].astype(F32)
    wts = dict(
        wpa=w_pa[l].astype(BF16), wpb=w_pb[l].astype(BF16), wo=w_o[l].astype(BF16),
        gain_ffn=norm_ffn[l][None, :], wq=peer_wq[l].astype(BF16),
        sk=peer_subkeys[l].reshape(2 * PEER_HEADS, PEER_NKEYS, PEER_HALF).astype(BF16),
        u=_pack_table(peer_u[l]), v=_pack_table(peer_v[l]),
        gain_final=norm_final.reshape(SUBLANES, LANES),
    )

    mq, mk, mlf, mv, msg, _, km, vm, _, _ = _proj(meta_tokens, gain_mix, lb, w_in_b, N_META)
    zero_state = jnp.zeros((1, HG_HEADS, HG_DK, HG_DK), F32)
    _, s_meta = _hgrn(mq, mk, mlf, mv, msg, hg_gain, zero_state, 1, N_META, N_META)

    xp = x_prompt.reshape(b * seq, D_MODEL)
    q, k, lf, v, sg, aq, ak, av, sga, sgb = _proj(xp, gain_mix, lb, w_in_b, _tile(b * seq, 256))
    oa, s_fin = _hgrn(q, k, lf, v, sg, hg_gain, s_meta, b, HG_CHUNK, _tile(seq, 512))
    ob = _swa_prompt(sinks, aq, ak, av, km, vm, b)
    y_prompt = _mix_and_ffn(xp, oa, ob, sga, sgb, wts).reshape(b, seq, D_MODEL)
    akr = ak.reshape(b, seq, AT_KV_HEADS, AT_HEAD_DIM)
    avr = av.reshape(b, seq, AT_KV_HEADS, AT_HEAD_DIM)
    kmr = jnp.broadcast_to(km.reshape(1, N_META, AT_KV_HEADS, AT_HEAD_DIM), (b, N_META, AT_KV_HEADS, AT_HEAD_DIM))
    vmr = jnp.broadcast_to(vm.reshape(1, N_META, AT_KV_HEADS, AT_HEAD_DIM), (b, N_META, AT_KV_HEADS, AT_HEAD_DIM))

    xs = x_sample.reshape(db * ds, D_MODEL)
    q, k, lf, v, sg, aq, aks, avs, sga, sgb = _proj(xs, gain_mix, lb, w_in_b, _tile(db * ds, 256))
    cs = SUBLANES
    pad = lambda a: jnp.pad(a.reshape(db, ds, HG_WIDTH), ((0, 0), (0, cs - ds), (0, 0))).reshape(db * cs, HG_WIDTH)
    oa, s_new = _hgrn(pad(q), pad(k), pad(lf), pad(v), pad(sg), hg_gain, state_hgrn[l].astype(F32), db, cs, cs)
    oa = oa.reshape(db, cs, HG_WIDTH)[:, :ds].reshape(db * ds, HG_WIDTH)
    ob = _swa_sample(sinks, aq.reshape(db, ds, AT_WIDTH), aks.reshape(db, ds, AT_KV_WIDTH),
                     avs.reshape(db, ds, AT_KV_WIDTH),
                     cache_win_k[l].reshape(db, WINDOW, AT_KV_WIDTH), cache_win_v[l].reshape(db, WINDOW, AT_KV_WIDTH),
                     cache_meta_k[l].reshape(db, N_META, AT_KV_WIDTH), cache_meta_v[l].reshape(db, N_META, AT_KV_WIDTH),
                     8).reshape(db * ds, AT_WIDTH)
    y_sample = _mix_and_ffn(xs, oa, ob, sga, sgb, wts).reshape(db, ds, D_MODEL)

    return (y_prompt, y_sample,
            akr[:, -WINDOW:][None], avr[:, -WINDOW:][None], kmr[None], vmr[None], s_fin[None],
            aks.reshape(1, db, ds, AT_KV_HEADS, AT_HEAD_DIM), avs.reshape(1, db, ds, AT_KV_HEADS, AT_HEAD_DIM),
            s_new[None])
```

```python
import functools
import math

import numpy as np
import jax
import jax.numpy as jnp
from jax import lax
from jax.experimental import pallas as pl
from jax.experimental.pallas import tpu as pltpu

F32 = jnp.float32
BF16 = jnp.bfloat16
EPS = 1e-6
D_MODEL = 1024
N_META = 16
HG_HEADS = 4
HG_DK = 128
HG_WIDTH = HG_HEADS * HG_DK
HG_CHUNK = 64
AT_HEADS = 8
AT_KV_HEADS = 2
AT_GROUP = AT_HEADS // AT_KV_HEADS
AT_HEAD_DIM = 64
AT_WIDTH = AT_HEADS * AT_HEAD_DIM
AT_KV_WIDTH = AT_KV_HEADS * AT_HEAD_DIM
AT_SCALE = AT_HEAD_DIM ** -0.5
WINDOW = 128
BLOCK = 128
PAST_LEN = 16384
PEER_HEADS = 8
PEER_NKEYS = 128
PEER_HALF = 128
PEER_TOPK = 16
PEER_PICKS = PEER_HEADS * PEER_TOPK
SUBLANES = 8
LANES = 128
PEER_UNROLL = 16
VMEM_LIMIT = 56 * 1024 * 1024

NT_DIMS = (((1,), (1,)), ((), ()))
TN_DIMS = (((0,), (0,)), ((), ()))


def _cparams(*sem):
    return pltpu.CompilerParams(dimension_semantics=sem, vmem_limit_bytes=VMEM_LIMIT)


def _tile(n, pref):
    return pref if n % pref == 0 else n


def _sigmoid(x):
    return 1.0 / (1.0 + jnp.exp(-x))


def _proj_kernel(x_ref, gain_ref, lb_ref, w_ref, q_ref, k_ref, lf_ref, v_ref, sg_ref,
                 aq_ref, ak_ref, av_ref, sga_ref, sgb_ref):
    x = x_ref[...]
    xn = x * lax.rsqrt(jnp.mean(x * x, axis=-1, keepdims=True) + EPS) * gain_ref[...]
    xb = xn.astype(BF16)

    def proj(a, b):
        return jnp.dot(xb, w_ref[:, a:b], preferred_element_type=F32)

    o = 0
    hq = proj(o, o + HG_WIDTH); o += HG_WIDTH
    q_ref[...] = hq * _sigmoid(hq)
    lb = lb_ref[...]
    f = lb + (1.0 - lb) * _sigmoid(proj(o, o + HG_WIDTH)); o += HG_WIDTH
    k_ref[...] = 1.0 - f
    lf_ref[...] = jnp.log(f)
    v_ref[...] = proj(o, o + HG_WIDTH); o += HG_WIDTH
    hg = proj(o, o + HG_WIDTH); o += HG_WIDTH
    sg_ref[...] = hg * _sigmoid(hg)
    aq_ref[...] = proj(o, o + AT_WIDTH) * AT_SCALE; o += AT_WIDTH
    ak_ref[...] = proj(o, o + AT_KV_WIDTH); o += AT_KV_WIDTH
    av_ref[...] = proj(o, o + AT_KV_WIDTH); o += AT_KV_WIDTH
    sga_ref[...] = _sigmoid(proj(o, o + D_MODEL)); o += D_MODEL
    sgb_ref[...] = _sigmoid(proj(o, o + D_MODEL))


def _proj(x, gain, lb, w_bf16, tm):
    n = x.shape[0]
    widths = (HG_WIDTH,) * 5 + (AT_WIDTH, AT_KV_WIDTH, AT_KV_WIDTH, D_MODEL, D_MODEL)
    row = lambda w: pl.BlockSpec((tm, w), lambda i: (i, 0))
    full = lambda a: pl.BlockSpec(a.shape, lambda i: (0,) * a.ndim)
    return pl.pallas_call(
        _proj_kernel,
        grid=(n // tm,),
        in_specs=[row(D_MODEL), full(gain), full(lb), full(w_bf16)],
        out_specs=[row(w) for w in widths],
        out_shape=[jax.ShapeDtypeStruct((n, w), F32) for w in widths],
        compiler_params=_cparams("parallel"),
        name="proj",
    )(x, gain, lb, w_bf16)


def _hgrn_consts(c):
    levels = int(round(math.log2(c)))
    t = np.arange(c)
    le = (t[None, :] <= t[:, None]).astype(np.float32)
    mats = [le]
    masks = []
    for l in range(levels):
        bs, half = 2 << l, 1 << l
        base = (t // bs) * bs
        bnd = base + half - 1
        mats.append(le - (t[None, :] <= bnd[:, None]).astype(np.float32))
        right = (t % bs) >= half
        masks.append((base[:, None] == base[None, :]) & right[:, None] & (~right)[None, :])
    masks.append(np.eye(c, dtype=bool))
    a = np.concatenate(mats, 0)
    a2 = np.concatenate([a, a], 1)
    return jnp.asarray(a2, BF16), jnp.asarray(np.stack(masks).astype(np.float32))


def _hgrn_kernel(a_ref, m_ref, q_ref, k_ref, lf_ref, v_ref, sg_ref, gain_ref, s0_ref, o_ref, sfin_ref, st_ref,
                 *, c, nchunk):
    levels = int(round(math.log2(c)))
    i = pl.program_id(1)

    @pl.when(i == 0)
    def _():
        for h in range(HG_HEADS):
            st_ref[h] = s0_ref[0, h].T

    gain = gain_ref[...]

    def chunk(ci, carry):
        r0 = pl.multiple_of(ci * c, c)
        for h in range(HG_HEADS):
            sl = (pl.ds(r0, c), slice(h * HG_DK, (h + 1) * HG_DK))
            q = q_ref[sl]
            k = k_ref[sl]
            v = v_ref[sl]
            lf = lf_ref[sl]
            st = st_ref[h]
            hi = lf.astype(BF16)
            lo = (lf - hi.astype(F32)).astype(BF16)
            r = jnp.dot(a_ref[...], jnp.concatenate([hi, lo], axis=0), preferred_element_type=F32)
            cum = r[0:c]
            qb = q.astype(BF16)
            kb = k.astype(BF16)
            vb = v.astype(BF16)
            o = lax.dot_general((q * jnp.exp(cum)).astype(BF16), st.astype(BF16), NT_DIMS,
                                preferred_element_type=F32)
            att = m_ref[levels] * lax.dot_general(qb, kb, NT_DIMS, preferred_element_type=F32)
            for l in range(levels):
                e = r[(l + 1) * c:(l + 2) * c]
                aq = (q * jnp.exp(jnp.minimum(e, 0.0))).astype(BF16)
                ak = (k * jnp.exp(jnp.minimum(-e, 0.0))).astype(BF16)
                att = att + m_ref[l] * lax.dot_general(aq, ak, NT_DIMS, preferred_element_type=F32)
            o = o + jnp.dot(att.astype(BF16), vb, preferred_element_type=F32)
            last = cum[c - 1:c]
            kdec = (k * jnp.exp(last - cum)).astype(BF16)
            st_ref[h] = st * jnp.exp(last) + lax.dot_general(vb, kdec, TN_DIMS, preferred_element_type=F32)
            on = o * lax.rsqrt(jnp.mean(o * o, axis=-1, keepdims=True) + EPS) * gain
            o_ref[sl] = on * sg_ref[sl]
        return carry

    lax.fori_loop(0, nchunk, chunk, 0)

    @pl.when(i == pl.num_programs(1) - 1)
    def _():
        for h in range(HG_HEADS):
            sfin_ref[0, h] = st_ref[h].T


def _hgrn(q, k, lf, v, sg, gain, s0, batch, c, ct):
    n = q.shape[0]
    t = n // batch
    steps = t // ct
    a2, masks = _hgrn_consts(c)
    row = pl.BlockSpec((ct, HG_WIDTH), lambda b, i: (b * steps + i, 0))
    full = lambda a: pl.BlockSpec(a.shape, lambda b, i: (0,) * a.ndim)
    shared = s0.shape[0] == 1
    sspec = pl.BlockSpec((1, HG_HEADS, HG_DK, HG_DK), (lambda b, i: (0, 0, 0, 0)) if shared else (lambda b, i: (b, 0, 0, 0)))
    return pl.pallas_call(
        functools.partial(_hgrn_kernel, c=c, nchunk=ct // c),
        grid=(batch, steps),
        in_specs=[full(a2), full(masks), row, row, row, row, row, full(gain), sspec],
        out_specs=[row, pl.BlockSpec((1, HG_HEADS, HG_DK, HG_DK), lambda b, i: (b, 0, 0, 0))],
        out_shape=[jax.ShapeDtypeStruct((n, HG_WIDTH), F32),
                   jax.ShapeDtypeStruct((batch, HG_HEADS, HG_DK, HG_DK), F32)],
        scratch_shapes=[pltpu.VMEM((HG_HEADS, HG_DK, HG_DK), F32)],
        compiler_params=_cparams("parallel", "arbitrary"),
        name="hgrn",
    )(a2, masks, q, k, lf, v, sg, gain, s0)


def _swa_prompt_kernel(sink_ref, q_ref, kc_ref, kp_ref, vc_ref, vp_ref, km_ref, vm_ref, o_ref):
    n = pl.program_id(1)
    qi = lax.broadcasted_iota(jnp.int32, (BLOCK, 2 * BLOCK), 0)
    sj = lax.broadcasted_iota(jnp.int32, (BLOCK, 2 * BLOCK), 1)
    dist = qi + BLOCK - sj
    valid = (dist >= 0) & (dist <= WINDOW) & ((sj >= BLOCK) | (n > 0))
    distf = dist.astype(F32)
    for kv in range(AT_KV_HEADS):
        ks = slice(kv * AT_HEAD_DIM, (kv + 1) * AT_HEAD_DIM)
        kband = jnp.concatenate([kp_ref[:, ks], kc_ref[:, ks]], axis=0).astype(BF16)
        vband = jnp.concatenate([vp_ref[:, ks], vc_ref[:, ks]], axis=0).astype(BF16)
        kmeta = km_ref[:, ks].astype(BF16)
        vmeta = vm_ref[:, ks].astype(BF16)
        for g in range(AT_GROUP):
            h = kv * AT_GROUP + g
            hs = slice(h * AT_HEAD_DIM, (h + 1) * AT_HEAD_DIM)
            slope = 2.0 ** (-8.0 * (h + 1) / AT_HEADS)
            qh = q_ref[:, hs].astype(BF16)
            lband = lax.dot_general(qh, kband, NT_DIMS, preferred_element_type=F32) - slope * distf
            lband = jnp.where(valid, lband, -jnp.inf)
            lmeta = lax.dot_general(qh, kmeta, NT_DIMS, preferred_element_type=F32)
            sink = sink_ref[h]
            m = jnp.maximum(jnp.maximum(jnp.max(lband, axis=-1, keepdims=True),
                                        jnp.max(lmeta, axis=-1, keepdims=True)), sink)
            eb = jnp.exp(lband - m)
            em = jnp.exp(lmeta - m)
            den = (jnp.sum(eb, axis=-1, keepdims=True) + jnp.sum(em, axis=-1, keepdims=True)
                   + jnp.exp(sink - m))
            o = (jnp.dot(eb.astype(BF16), vband, preferred_element_type=F32)
                 + jnp.dot(em.astype(BF16), vmeta, preferred_element_type=F32))
            o_ref[:, hs] = o / den


def _swa_prompt(sinks, aq, ak, av, km, vm, batch):
    n = aq.shape[0]
    nb = n // batch // BLOCK
    cur = lambda w: pl.BlockSpec((BLOCK, w), lambda b, i: (b * nb + i, 0))
    prev = lambda w: pl.BlockSpec((BLOCK, w), lambda b, i: (b * nb + jnp.maximum(i - 1, 0), 0))
    meta = pl.BlockSpec((N_META, AT_KV_WIDTH), lambda b, i: (0, 0))
    return pl.pallas_call(
        _swa_prompt_kernel,
        grid=(batch, nb),
        in_specs=[pl.BlockSpec(memory_space=pltpu.SMEM), cur(AT_WIDTH), cur(AT_KV_WIDTH), prev(AT_KV_WIDTH),
                  cur(AT_KV_WIDTH), prev(AT_KV_WIDTH), meta, meta],
        out_specs=cur(AT_WIDTH),
        out_shape=jax.ShapeDtypeStruct((n, AT_WIDTH), F32),
        compiler_params=_cparams("parallel", "parallel"),
        name="swa_prompt",
    )(sinks, aq, ak, ak, av, av, km, vm)


def _swa_sample_kernel(sink_ref, q_ref, kn_ref, vn_ref, kw_ref, vw_ref, km_ref, vm_ref, o_ref, *, bg, ds):
    rows = AT_GROUP * ds
    nk = WINDOW + ds
    ri = lax.broadcasted_iota(jnp.int32, (rows, nk), 0)
    sj = lax.broadcasted_iota(jnp.int32, (rows, nk), 1)
    qpos = PAST_LEN + ri % ds
    kpos = PAST_LEN - WINDOW + sj
    dist = qpos - kpos
    valid = (dist >= 0) & (dist <= WINDOW) & (kpos >= N_META)
    distf = dist.astype(F32)
    gi = lax.broadcasted_iota(jnp.int32, (rows, 1), 0) // ds
    for kv in range(AT_KV_HEADS):
        ks = slice(kv * AT_HEAD_DIM, (kv + 1) * AT_HEAD_DIM)
        slope = jnp.exp2(-(gi + (kv * AT_GROUP + 1)).astype(F32) * (8.0 / AT_HEADS))
        sink = jnp.zeros((rows, 1), F32)
        for g in range(AT_GROUP):
            sink = jnp.where(gi == g, sink_ref[kv * AT_GROUP + g], sink)
        for b in range(bg):
            kall = jnp.concatenate([kw_ref[b, :, ks], kn_ref[b, :, ks]], axis=0).astype(BF16)
            vall = jnp.concatenate([vw_ref[b, :, ks], vn_ref[b, :, ks]], axis=0).astype(BF16)
            kmeta = km_ref[b, :, ks].astype(BF16)
            vmeta = vm_ref[b, :, ks].astype(BF16)
            qs = jnp.concatenate(
                [q_ref[b, :, (kv * AT_GROUP + g) * AT_HEAD_DIM:(kv * AT_GROUP + g + 1) * AT_HEAD_DIM]
                 for g in range(AT_GROUP)], axis=0).astype(BF16)
            lw = lax.dot_general(qs, kall, NT_DIMS, preferred_element_type=F32) - slope * distf
            lw = jnp.where(valid, lw, -jnp.inf)
            lm = lax.dot_general(qs, kmeta, NT_DIMS, preferred_element_type=F32)
            m = jnp.maximum(jnp.maximum(jnp.max(lw, axis=-1, keepdims=True),
                                        jnp.max(lm, axis=-1, keepdims=True)), sink)
            ew = jnp.exp(lw - m)
            em = jnp.exp(lm - m)
            den = (jnp.sum(ew, axis=-1, keepdims=True) + jnp.sum(em, axis=-1, keepdims=True)
                   + jnp.exp(sink - m))
            o = (jnp.dot(ew.astype(BF16), vall, preferred_element_type=F32)
                 + jnp.dot(em.astype(BF16), vmeta, preferred_element_type=F32)) / den
            for g in range(AT_GROUP):
                h = kv * AT_GROUP + g
                o_ref[b, :, h * AT_HEAD_DIM:(h + 1) * AT_HEAD_DIM] = o[g * ds:(g + 1) * ds]


def _swa_sample(sinks, aq, ak, av, win_k, win_v, meta_k, meta_v, bg):
    b, ds = aq.shape[:2]
    spec = lambda a: pl.BlockSpec((bg,) + a.shape[1:], lambda i: (i, 0, 0))
    args = (aq, ak, av, win_k, win_v, meta_k, meta_v)
    return pl.pallas_call(
        functools.partial(_swa_sample_kernel, bg=bg, ds=ds),
        grid=(b // bg,),
        in_specs=[pl.BlockSpec(memory_space=pltpu.SMEM)] + [spec(a) for a in args],
        out_specs=spec(aq),
        out_shape=jax.ShapeDtypeStruct(aq.shape, F32),
        compiler_params=_cparams("parallel"),
        name="swa_sample",
    )(sinks, *args)


def _merge_kernel(x_ref, oa_ref, ob_ref, sga_ref, sgb_ref, wpa_ref, wpb_ref, wo_ref, gain_ref, wq_ref, sk_ref,
                  h2_ref, xn_ref, sc_ref):
    pa = jnp.dot(oa_ref[...].astype(BF16), wpa_ref[...], preferred_element_type=F32)
    pb = jnp.dot(ob_ref[...].astype(BF16), wpb_ref[...], preferred_element_type=F32)
    m = sga_ref[...] * pa + sgb_ref[...] * pb
    h2 = x_ref[...] + jnp.dot(m.astype(BF16), wo_ref[...], preferred_element_type=F32)
    h2_ref[...] = h2
    xn = h2 * lax.rsqrt(jnp.mean(h2 * h2, axis=-1, keepdims=True) + EPS) * gain_ref[...]
    xn_ref[...] = xn
    qp = jnp.dot(xn.astype(BF16), wq_ref[...], preferred_element_type=F32).astype(BF16)
    for hc in range(2 * PEER_HEADS):
        sc_ref[hc] = lax.dot_general(sk_ref[hc], qp[:, hc * PEER_HALF:(hc + 1) * PEER_HALF], NT_DIMS,
                                     preferred_element_type=F32)


def _merge(x, oa, ob, sga, sgb, wpa, wpb, wo, gain, wq, sk, tm):
    n = x.shape[0]
    row = lambda w: pl.BlockSpec((tm, w), lambda i: (i, 0))
    full = lambda a: pl.BlockSpec(a.shape, lambda i: (0,) * a.ndim)
    return pl.pallas_call(
        _merge_kernel,
        grid=(n // tm,),
        in_specs=[row(D_MODEL), row(HG_WIDTH), row(AT_WIDTH), row(D_MODEL), row(D_MODEL),
                  full(wpa), full(wpb), full(wo), full(gain), full(wq), full(sk)],
        out_specs=[row(D_MODEL), row(D_MODEL),
                   pl.BlockSpec((2 * PEER_HEADS, PEER_NKEYS, tm), lambda i: (0, 0, i))],
        out_shape=[jax.ShapeDtypeStruct((n, D_MODEL), F32), jax.ShapeDtypeStruct((n, D_MODEL), F32),
                   jax.ShapeDtypeStruct((2 * PEER_HEADS, PEER_NKEYS, n), F32)],
        compiler_params=_cparams("parallel"),
        name="merge",
    )(x, oa, ob, sga, sgb, wpa, wpb, wo, gain, wq, sk)


def _top16(s):
    rows = s.shape[0]
    io = lax.broadcasted_iota(jnp.int32, s.shape, 0)
    vals, idxs = [], []
    for _ in range(PEER_TOPK):
        m = jnp.max(s, axis=0, keepdims=True)
        idx = jnp.min(jnp.where(s == m, io, rows), axis=0, keepdims=True)
        vals.append(m)
        idxs.append(idx)
        s = jnp.where(io == idx, -jnp.inf, s)
    return vals, idxs


def _topk_kernel(s_ref, eidx_ref, gate_ref):
    io16 = lax.broadcasted_iota(jnp.int32, (PEER_TOPK, s_ref.shape[2]), 0)
    for h in range(PEER_HEADS):
        v1, i1 = _top16(s_ref[2 * h])
        v2, i2 = _top16(s_ref[2 * h + 1])
        v2s = jnp.concatenate(v2, axis=0)
        i1s = jnp.concatenate(i1, axis=0)
        i2s = jnp.concatenate(i2, axis=0)
        cand = jnp.concatenate([v1[a] + v2s for a in range(PEER_TOPK)], axis=0)
        ctop, cpos = _top16(cand)
        es = []
        for j in range(PEER_TOPK):
            a = cpos[j] // PEER_TOPK
            b = cpos[j] % PEER_TOPK
            e1 = jnp.sum(jnp.where(io16 == a, i1s, 0), axis=0, keepdims=True)
            e2 = jnp.sum(jnp.where(io16 == b, i2s, 0), axis=0, keepdims=True)
            es.append(e1 * PEER_NKEYS + e2)
        ts = jnp.concatenate(ctop, axis=0)
        ex = jnp.exp(ts - ctop[0])
        sl = slice(h * PEER_TOPK, (h + 1) * PEER_TOPK)
        gate_ref[sl, :] = ex / jnp.sum(ex, axis=0, keepdims=True)
        eidx_ref[sl, :] = jnp.concatenate(es, axis=0)


def _topk(scores_t, tk):
    n = scores_t.shape[2]
    return pl.pallas_call(
        _topk_kernel,
        grid=(n // tk,),
        in_specs=[pl.BlockSpec((2 * PEER_HEADS, PEER_NKEYS, tk), lambda i: (0, 0, i))],
        out_specs=[pl.BlockSpec((PEER_PICKS, tk), lambda i: (0, i))] * 2,
        out_shape=[jax.ShapeDtypeStruct((PEER_PICKS, n), jnp.int32),
                   jax.ShapeDtypeStruct((PEER_PICKS, n), F32)],
        compiler_params=_cparams("parallel"),
        name="topk",
    )(scores_t)


def _table_rows(tab):
    return tab.astype(BF16).reshape(tab.shape[0], SUBLANES, LANES)


_BITREV = tuple(int(format(i, "03b")[::-1], 2) for i in range(SUBLANES))


def _fold_sublanes(a, b, half):
    s = lax.broadcasted_iota(jnp.int32, (SUBLANES, LANES), 0)
    low = (s & half) == 0
    if 2 * half < SUBLANES:
        b = pltpu.roll(b, half, axis=0)
    return jnp.where(low, a, b) + pltpu.roll(jnp.where(low, b, a), SUBLANES - half, axis=0)


def _peer_a_kernel(idx_ref, x_ref, gate_ref, tab_ref, w_ref, q_ref, *, tb):
    def tok(t, carry):
        xv = x_ref[t]

        def group(g, c2):
            for sub in range(PEER_UNROLL // SUBLANES):
                r0 = g * PEER_UNROLL + sub * SUBLANES
                ids = idx_ref.at[pl.ds(t * PEER_PICKS + r0, SUBLANES)]
                tiles = [tab_ref[ids[_BITREV[i]]].astype(F32) * xv for i in range(SUBLANES)]
                half = SUBLANES // 2
                while len(tiles) > 1:
                    tiles = [_fold_sublanes(tiles[i], tiles[i + 1], half) for i in range(0, len(tiles), 2)]
                    half //= 2
                q_ref[t, pl.ds(pl.multiple_of(r0, SUBLANES), SUBLANES), :] = tiles[0]
            return c2

        lax.fori_loop(0, PEER_PICKS // PEER_UNROLL, group, 0)
        return carry

    lax.fori_loop(0, tb, tok, 0)

    lane = lax.broadcasted_iota(jnp.int32, (PEER_PICKS, tb), 1)

    def reduce(tg, h):
        for i in range(SUBLANES):
            t = tg * SUBLANES + i
            h = jnp.where(lane == t, jnp.sum(q_ref[t], axis=1, keepdims=True), h)
        return h

    h = lax.fori_loop(0, tb // SUBLANES, reduce, jnp.zeros((PEER_PICKS, tb), F32))
    w_ref[...] = gate_ref[...] * (0.5 * h * (1.0 + lax.erf(h * math.sqrt(0.5))))


def _peer_a(eidx_flat, xn3, gate_t, tab, tb):
    n = xn3.shape[0]
    return pl.pallas_call(
        functools.partial(_peer_a_kernel, tb=tb),
        grid=(n // tb,),
        in_specs=[pl.BlockSpec((tb * PEER_PICKS,), lambda i: (i,), memory_space=pltpu.SMEM),
                  pl.BlockSpec((tb, SUBLANES, LANES), lambda i: (i, 0, 0)),
                  pl.BlockSpec((PEER_PICKS, tb), lambda i: (0, i)),
                  pl.BlockSpec(tab.shape, lambda i: (0, 0, 0), pipeline_mode=pl.Buffered(1))],
        out_specs=pl.BlockSpec((PEER_PICKS, tb), lambda i: (0, i)),
        out_shape=jax.ShapeDtypeStruct((PEER_PICKS, n), F32),
        scratch_shapes=[pltpu.VMEM((tb, PEER_PICKS, LANES), F32)],
        compiler_params=_cparams("arbitrary"),
        name="peer_a",
    )(eidx_flat, xn3, gate_t, tab)


def _peer_b_kernel(idx_ref, wt_ref, h2_ref, gain_ref, tab_ref, y_ref, wb_ref, *, tb):
    nacc = 4
    wt = wt_ref[...]

    def spread(tg, carry):
        for i in range(SUBLANES):
            t = tg * SUBLANES + i
            wb_ref[t] = jnp.take_along_axis(wt, jnp.full((PEER_PICKS, LANES), t, jnp.int32), axis=1)
        return carry

    lax.fori_loop(0, tb // SUBLANES, spread, 0)

    def tok(t, carry):
        def group(g, acc):
            acc = list(acc)
            for sub in range(PEER_UNROLL // SUBLANES):
                r0 = g * PEER_UNROLL + sub * SUBLANES
                base = t * PEER_PICKS + r0
                wg = wb_ref[t, pl.ds(pl.multiple_of(r0, SUBLANES), SUBLANES), :]
                for i in range(SUBLANES):
                    tile = tab_ref[idx_ref[base + i]].astype(F32)
                    acc[i % nacc] = acc[i % nacc] + wg[i:i + 1, :] * tile
            return tuple(acc)

        acc = lax.fori_loop(0, PEER_PICKS // PEER_UNROLL, group,
                            tuple(jnp.zeros((SUBLANES, LANES), F32) for _ in range(nacc)))
        y_ref[t] = h2_ref[t] + ((acc[0] + acc[1]) + (acc[2] + acc[3]))
        return carry

    lax.fori_loop(0, tb, tok, 0)

    y = y_ref[...]
    ms = jnp.sum(jnp.sum(y * y, axis=2, keepdims=True), axis=1, keepdims=True) * (1.0 / D_MODEL)
    y_ref[...] = y * lax.rsqrt(ms + EPS) * gain_ref[...][None]


def _peer_b(eidx_flat, w_t, h23, gain3, tab, tb):
    n = h23.shape[0]
    tok3 = pl.BlockSpec((tb, SUBLANES, LANES), lambda i: (i, 0, 0))
    return pl.pallas_call(
        functools.partial(_peer_b_kernel, tb=tb),
        grid=(n // tb,),
        in_specs=[pl.BlockSpec((tb * PEER_PICKS,), lambda i: (i,), memory_space=pltpu.SMEM),
                  pl.BlockSpec((PEER_PICKS, tb), lambda i: (0, i)),
                  tok3, pl.BlockSpec(gain3.shape, lambda i: (0, 0)),
                  pl.BlockSpec(tab.shape, lambda i: (0, 0, 0), pipeline_mode=pl.Buffered(1))],
        out_specs=tok3,
        out_shape=jax.ShapeDtypeStruct(h23.shape, F32),
        scratch_shapes=[pltpu.VMEM((tb, PEER_PICKS, LANES), F32)],
        compiler_params=_cparams("arbitrary"),
        name="peer_b",
    )(eidx_flat, w_t, h23, gain3, tab)


def _mix_and_ffn(x, oa, ob, sga, sgb, wts):
    n = x.shape[0]
    tm, tk, tb = _tile(n, 256), _tile(n, LANES), _tile(n, LANES)
    h2, xn, scores_t = _merge(x, oa, ob, sga, sgb, wts["wpa"], wts["wpb"], wts["wo"], wts["gain_ffn"],
                              wts["wq"], wts["sk"], tm)
    eidx_t, gate_t = _topk(scores_t, tk)
    eidx_flat = eidx_t.T.reshape(n * PEER_PICKS)
    w_t = _peer_a(eidx_flat, xn.reshape(n, SUBLANES, LANES), gate_t, wts["u"], tb)
    y3 = _peer_b(eidx_flat, w_t, h2.reshape(n, SUBLANES, LANES), wts["gain_final"], wts["v"], tb)
    return y3.reshape(n, D_MODEL)


def kernel(x_prompt, x_sample, cache_win_k, cache_win_v, cache_meta_k, cache_meta_v, state_hgrn, meta_tokens,
           norm_mix, w_in, hg_lb, hg_norm, attn_sinks, w_pa, w_pb, w_o, norm_ffn, peer_wq, peer_subkeys,
           peer_u, peer_v, norm_final):
    depth = w_in.shape[0]
    assert depth == 1, "single-layer step only"
    b, seq, _ = x_prompt.shape
    db, ds, _ = x_sample.shape
    l = 0
    lb = jnp.cumsum(jax.nn.softmax(hg_lb.astype(F32), axis=0), axis=0)[l][None, :]
    gain_mix = norm_mix[l][None, :]
    w_in_b = w_in[l].astype(BF16)
    hg_gain = hg_norm[l][None, :]
    sinks = attn_sinks[l].astype(F32)
    wts = dict(
        wpa=w_pa[l].astype(BF16), wpb=w_pb[l].astype(BF16), wo=w_o[l].astype(BF16),
        gain_ffn=norm_ffn[l][None, :], wq=peer_wq[l].astype(BF16),
        sk=peer_subkeys[l].reshape(2 * PEER_HEADS, PEER_NKEYS, PEER_HALF).astype(BF16),
        u=_table_rows(peer_u[l]), v=_table_rows(peer_v[l]),
        gain_final=norm_final.reshape(SUBLANES, LANES),
    )

    mq, mk, mlf, mv, msg, _, km, vm, _, _ = _proj(meta_tokens, gain_mix, lb, w_in_b, N_META)
    zero_state = jnp.zeros((1, HG_HEADS, HG_DK, HG_DK), F32)
    _, s_meta = _hgrn(mq, mk, mlf, mv, msg, hg_gain, zero_state, 1, N_META, N_META)

    xp = x_prompt.reshape(b * seq, D_MODEL)
    q, k, lf, v, sg, aq, ak, av, sga, sgb = _proj(xp, gain_mix, lb, w_in_b, _tile(b * seq, 256))
    oa, s_fin = _hgrn(q, k, lf, v, sg, hg_gain, s_meta, b, HG_CHUNK, _tile(seq, 512))
    ob = _swa_prompt(sinks, aq, ak, av, km, vm, b)
    y_prompt = _mix_and_ffn(xp, oa, ob, sga, sgb, wts).reshape(b, seq, D_MODEL)
    akr = ak.reshape(b, seq, AT_KV_HEADS, AT_HEAD_DIM)
    avr = av.reshape(b, seq, AT_KV_HEADS, AT_HEAD_DIM)
    kmr = jnp.broadcast_to(km.reshape(1, N_META, AT_KV_HEADS, AT_HEAD_DIM), (b, N_META, AT_KV_HEADS, AT_HEAD_DIM))
    vmr = jnp.broadcast_to(vm.reshape(1, N_META, AT_KV_HEADS, AT_HEAD_DIM), (b, N_META, AT_KV_HEADS, AT_HEAD_DIM))

    xs = x_sample.reshape(db * ds, D_MODEL)
    q, k, lf, v, sg, aq, aks, avs, sga, sgb = _proj(xs, gain_mix, lb, w_in_b, _tile(db * ds, 256))
    cs = SUBLANES
    pad = lambda a: jnp.pad(a.reshape(db, ds, HG_WIDTH), ((0, 0), (0, cs - ds), (0, 0))).reshape(db * cs, HG_WIDTH)
    oa, s_new = _hgrn(pad(q), pad(k), pad(lf), pad(v), pad(sg), hg_gain, state_hgrn[l].astype(F32), db, cs, cs)
    oa = oa.reshape(db, cs, HG_WIDTH)[:, :ds].reshape(db * ds, HG_WIDTH)
    ob = _swa_sample(sinks, aq.reshape(db, ds, AT_WIDTH), aks.reshape(db, ds, AT_KV_WIDTH),
                     avs.reshape(db, ds, AT_KV_WIDTH),
                     cache_win_k[l].reshape(db, WINDOW, AT_KV_WIDTH), cache_win_v[l].reshape(db, WINDOW, AT_KV_WIDTH),
                     cache_meta_k[l].reshape(db, N_META, AT_KV_WIDTH), cache_meta_v[l].reshape(db, N_META, AT_KV_WIDTH),
                     8).reshape(db * ds, AT_WIDTH)
    y_sample = _mix_and_ffn(xs, oa, ob, sga, sgb, wts).reshape(db, ds, D_MODEL)

    return (y_prompt, y_sample,
            akr[:, -WINDOW:][None], avr[:, -WINDOW:][None], kmr[None], vmr[None], s_fin[None],
            aks.reshape(1, db, ds, AT_KV_HEADS, AT_HEAD_DIM), avs.reshape(1, db, ds, AT_KV_HEADS, AT_HEAD_DIM),
            s_new[None])
```

```python
import functools
import math

import numpy as np
import jax
import jax.numpy as jnp
from jax import lax
from jax.experimental import pallas as pl
from jax.experimental.pallas import tpu as pltpu

F32 = jnp.float32
BF16 = jnp.bfloat16
EPS = 1e-6
D_MODEL = 1024
N_META = 16
HG_HEADS = 4
HG_DK = 128
HG_WIDTH = HG_HEADS * HG_DK
HG_CHUNK = 64
AT_HEADS = 8
AT_KV_HEADS = 2
AT_GROUP = AT_HEADS // AT_KV_HEADS
AT_HEAD_DIM = 64
AT_WIDTH = AT_HEADS * AT_HEAD_DIM
AT_KV_WIDTH = AT_KV_HEADS * AT_HEAD_DIM
AT_SCALE = AT_HEAD_DIM ** -0.5
WINDOW = 128
BLOCK = 128
PAST_LEN = 16384
PEER_HEADS = 8
PEER_NKEYS = 128
PEER_HALF = 128
PEER_TOPK = 16
PEER_PICKS = PEER_HEADS * PEER_TOPK
PEER_OCTETS = PEER_PICKS // 8
SUBLANES = 8
LANES = 128
PEER_UNROLL = 64
PEER_UNROLL_B = 32
PEER_SLOTS = 2
VMEM_LIMIT = 56 * 1024 * 1024

NT_DIMS = (((1,), (1,)), ((), ()))
TN_DIMS = (((0,), (0,)), ((), ()))


def _cparams(*sem):
    return pltpu.CompilerParams(dimension_semantics=sem, vmem_limit_bytes=VMEM_LIMIT)


def _tile(n, pref):
    return pref if n % pref == 0 else n


def _sigmoid(x):
    return 1.0 / (1.0 + jnp.exp(-x))


def _proj_kernel(x_ref, gain_ref, lb_ref, w_ref, q_ref, k_ref, lf_ref, v_ref, sg_ref,
                 aq_ref, ak_ref, av_ref, sga_ref, sgb_ref):
    x = x_ref[...]
    xn = x * lax.rsqrt(jnp.mean(x * x, axis=-1, keepdims=True) + EPS) * gain_ref[...]
    xb = xn.astype(BF16)

    def proj(a, b):
        return jnp.dot(xb, w_ref[:, a:b], preferred_element_type=F32)

    o = 0
    hq = proj(o, o + HG_WIDTH); o += HG_WIDTH
    q_ref[...] = hq * _sigmoid(hq)
    lb = lb_ref[...]
    f = lb + (1.0 - lb) * _sigmoid(proj(o, o + HG_WIDTH)); o += HG_WIDTH
    k_ref[...] = 1.0 - f
    lf_ref[...] = jnp.log(f)
    v_ref[...] = proj(o, o + HG_WIDTH); o += HG_WIDTH
    hg = proj(o, o + HG_WIDTH); o += HG_WIDTH
    sg_ref[...] = hg * _sigmoid(hg)
    aq_ref[...] = proj(o, o + AT_WIDTH) * AT_SCALE; o += AT_WIDTH
    ak_ref[...] = proj(o, o + AT_KV_WIDTH); o += AT_KV_WIDTH
    av_ref[...] = proj(o, o + AT_KV_WIDTH); o += AT_KV_WIDTH
    sga_ref[...] = _sigmoid(proj(o, o + D_MODEL)); o += D_MODEL
    sgb_ref[...] = _sigmoid(proj(o, o + D_MODEL))


def _proj(x, gain, lb, w_bf16, tm):
    n = x.shape[0]
    widths = (HG_WIDTH,) * 5 + (AT_WIDTH, AT_KV_WIDTH, AT_KV_WIDTH, D_MODEL, D_MODEL)
    row = lambda w: pl.BlockSpec((tm, w), lambda i: (i, 0))
    full = lambda a: pl.BlockSpec(a.shape, lambda i: (0,) * a.ndim)
    return pl.pallas_call(
        _proj_kernel,
        grid=(n // tm,),
        in_specs=[row(D_MODEL), full(gain), full(lb), full(w_bf16)],
        out_specs=[row(w) for w in widths],
        out_shape=[jax.ShapeDtypeStruct((n, w), F32) for w in widths],
        compiler_params=_cparams("parallel"),
        name="proj",
    )(x, gain, lb, w_bf16)


def _hgrn_consts(c):
    levels = int(round(math.log2(c)))
    t = np.arange(c)
    le = (t[None, :] <= t[:, None]).astype(np.float32)
    mats = [le]
    masks = []
    for l in range(levels):
        bs, half = 2 << l, 1 << l
        base = (t // bs) * bs
        bnd = base + half - 1
        mats.append(le - (t[None, :] <= bnd[:, None]).astype(np.float32))
        right = (t % bs) >= half
        masks.append((base[:, None] == base[None, :]) & right[:, None] & (~right)[None, :])
    masks.append(np.eye(c, dtype=bool))
    a = np.concatenate(mats, 0)
    a2 = np.concatenate([a, a], 1)
    return jnp.asarray(a2, BF16), jnp.asarray(np.stack(masks).astype(np.float32))


def _hgrn_kernel(a_ref, m_ref, q_ref, k_ref, lf_ref, v_ref, sg_ref, gain_ref, s0_ref, o_ref, sfin_ref, st_ref,
                 *, c, nchunk):
    levels = int(round(math.log2(c)))
    i = pl.program_id(1)

    @pl.when(i == 0)
    def _():
        for h in range(HG_HEADS):
            st_ref[h] = s0_ref[0, h].T

    gain = gain_ref[...]

    def chunk(ci, carry):
        r0 = pl.multiple_of(ci * c, c)
        for h in range(HG_HEADS):
            sl = (pl.ds(r0, c), slice(h * HG_DK, (h + 1) * HG_DK))
            q = q_ref[sl]
            k = k_ref[sl]
            v = v_ref[sl]
            lf = lf_ref[sl]
            st = st_ref[h]
            hi = lf.astype(BF16)
            lo = (lf - hi.astype(F32)).astype(BF16)
            r = jnp.dot(a_ref[...], jnp.concatenate([hi, lo], axis=0), preferred_element_type=F32)
            cum = r[0:c]
            qb = q.astype(BF16)
            kb = k.astype(BF16)
            vb = v.astype(BF16)
            o = lax.dot_general((q * jnp.exp(cum)).astype(BF16), st.astype(BF16), NT_DIMS,
                                preferred_element_type=F32)
            att = m_ref[levels] * lax.dot_general(qb, kb, NT_DIMS, preferred_element_type=F32)
            for l in range(levels):
                e = r[(l + 1) * c:(l + 2) * c]
                aq = (q * jnp.exp(jnp.minimum(e, 0.0))).astype(BF16)
                ak = (k * jnp.exp(jnp.minimum(-e, 0.0))).astype(BF16)
                att = att + m_ref[l] * lax.dot_general(aq, ak, NT_DIMS, preferred_element_type=F32)
            o = o + jnp.dot(att.astype(BF16), vb, preferred_element_type=F32)
            last = cum[c - 1:c]
            kdec = (k * jnp.exp(last - cum)).astype(BF16)
            st_ref[h] = st * jnp.exp(last) + lax.dot_general(vb, kdec, TN_DIMS, preferred_element_type=F32)
            on = o * lax.rsqrt(jnp.mean(o * o, axis=-1, keepdims=True) + EPS) * gain
            o_ref[sl] = on * sg_ref[sl]
        return carry

    lax.fori_loop(0, nchunk, chunk, 0)

    @pl.when(i == pl.num_programs(1) - 1)
    def _():
        for h in range(HG_HEADS):
            sfin_ref[0, h] = st_ref[h].T


def _hgrn(q, k, lf, v, sg, gain, s0, batch, c, ct):
    n = q.shape[0]
    t = n // batch
    steps = t // ct
    a2, masks = _hgrn_consts(c)
    row = pl.BlockSpec((ct, HG_WIDTH), lambda b, i: (b * steps + i, 0))
    full = lambda a: pl.BlockSpec(a.shape, lambda b, i: (0,) * a.ndim)
    shared = s0.shape[0] == 1
    sspec = pl.BlockSpec((1, HG_HEADS, HG_DK, HG_DK), (lambda b, i: (0, 0, 0, 0)) if shared else (lambda b, i: (b, 0, 0, 0)))
    return pl.pallas_call(
        functools.partial(_hgrn_kernel, c=c, nchunk=ct // c),
        grid=(batch, steps),
        in_specs=[full(a2), full(masks), row, row, row, row, row, full(gain), sspec],
        out_specs=[row, pl.BlockSpec((1, HG_HEADS, HG_DK, HG_DK), lambda b, i: (b, 0, 0, 0))],
        out_shape=[jax.ShapeDtypeStruct((n, HG_WIDTH), F32),
                   jax.ShapeDtypeStruct((batch, HG_HEADS, HG_DK, HG_DK), F32)],
        scratch_shapes=[pltpu.VMEM((HG_HEADS, HG_DK, HG_DK), F32)],
        compiler_params=_cparams("parallel", "arbitrary"),
        name="hgrn",
    )(a2, masks, q, k, lf, v, sg, gain, s0)


def _swa_prompt_kernel(sink_ref, q_ref, kc_ref, kp_ref, vc_ref, vp_ref, km_ref, vm_ref, o_ref):
    n = pl.program_id(1)
    qi = lax.broadcasted_iota(jnp.int32, (BLOCK, 2 * BLOCK), 0)
    sj = lax.broadcasted_iota(jnp.int32, (BLOCK, 2 * BLOCK), 1)
    dist = qi + BLOCK - sj
    valid = (dist >= 0) & (dist <= WINDOW) & ((sj >= BLOCK) | (n > 0))
    distf = dist.astype(F32)
    for kv in range(AT_KV_HEADS):
        ks = slice(kv * AT_HEAD_DIM, (kv + 1) * AT_HEAD_DIM)
        kband = jnp.concatenate([kp_ref[:, ks], kc_ref[:, ks]], axis=0).astype(BF16)
        vband = jnp.concatenate([vp_ref[:, ks], vc_ref[:, ks]], axis=0).astype(BF16)
        kmeta = km_ref[:, ks].astype(BF16)
        vmeta = vm_ref[:, ks].astype(BF16)
        for g in range(AT_GROUP):
            h = kv * AT_GROUP + g
            hs = slice(h * AT_HEAD_DIM, (h + 1) * AT_HEAD_DIM)
            slope = 2.0 ** (-8.0 * (h + 1) / AT_HEADS)
            qh = q_ref[:, hs].astype(BF16)
            lband = lax.dot_general(qh, kband, NT_DIMS, preferred_element_type=F32) - slope * distf
            lband = jnp.where(valid, lband, -jnp.inf)
            lmeta = lax.dot_general(qh, kmeta, NT_DIMS, preferred_element_type=F32)
            sink = sink_ref[h]
            m = jnp.maximum(jnp.maximum(jnp.max(lband, axis=-1, keepdims=True),
                                        jnp.max(lmeta, axis=-1, keepdims=True)), sink)
            eb = jnp.exp(lband - m)
            em = jnp.exp(lmeta - m)
            den = (jnp.sum(eb, axis=-1, keepdims=True) + jnp.sum(em, axis=-1, keepdims=True)
                   + jnp.exp(sink - m))
            o = (jnp.dot(eb.astype(BF16), vband, preferred_element_type=F32)
                 + jnp.dot(em.astype(BF16), vmeta, preferred_element_type=F32))
            o_ref[:, hs] = o / den


def _swa_prompt(sinks, aq, ak, av, km, vm, batch):
    n = aq.shape[0]
    nb = n // batch // BLOCK
    cur = lambda w: pl.BlockSpec((BLOCK, w), lambda b, i: (b * nb + i, 0))
    prev = lambda w: pl.BlockSpec((BLOCK, w), lambda b, i: (b * nb + jnp.maximum(i - 1, 0), 0))
    meta = pl.BlockSpec((N_META, AT_KV_WIDTH), lambda b, i: (0, 0))
    return pl.pallas_call(
        _swa_prompt_kernel,
        grid=(batch, nb),
        in_specs=[pl.BlockSpec(memory_space=pltpu.SMEM), cur(AT_WIDTH), cur(AT_KV_WIDTH), prev(AT_KV_WIDTH),
                  cur(AT_KV_WIDTH), prev(AT_KV_WIDTH), meta, meta],
        out_specs=cur(AT_WIDTH),
        out_shape=jax.ShapeDtypeStruct((n, AT_WIDTH), F32),
        compiler_params=_cparams("parallel", "parallel"),
        name="swa_prompt",
    )(sinks, aq, ak, ak, av, av, km, vm)


def _swa_sample_kernel(sink_ref, q_ref, kn_ref, vn_ref, kw_ref, vw_ref, km_ref, vm_ref, o_ref, *, bg, ds):
    rows = AT_GROUP * ds
    nk = WINDOW + ds
    ri = lax.broadcasted_iota(jnp.int32, (rows, nk), 0)
    sj = lax.broadcasted_iota(jnp.int32, (rows, nk), 1)
    qpos = PAST_LEN + ri % ds
    kpos = PAST_LEN - WINDOW + sj
    dist = qpos - kpos
    valid = (dist >= 0) & (dist <= WINDOW) & (kpos >= N_META)
    distf = dist.astype(F32)
    gi = lax.broadcasted_iota(jnp.int32, (rows, 1), 0) // ds
    for kv in range(AT_KV_HEADS):
        ks = slice(kv * AT_HEAD_DIM, (kv + 1) * AT_HEAD_DIM)
        slope = jnp.exp2(-(gi + (kv * AT_GROUP + 1)).astype(F32) * (8.0 / AT_HEADS))
        sink = jnp.zeros((rows, 1), F32)
        for g in range(AT_GROUP):
            sink = jnp.where(gi == g, sink_ref[kv * AT_GROUP + g], sink)
        for b in range(bg):
            kall = jnp.concatenate([kw_ref[b, :, ks], kn_ref[b, :, ks]], axis=0).astype(BF16)
            vall = jnp.concatenate([vw_ref[b, :, ks], vn_ref[b, :, ks]], axis=0).astype(BF16)
            kmeta = km_ref[b, :, ks].astype(BF16)
            vmeta = vm_ref[b, :, ks].astype(BF16)
            qs = jnp.concatenate(
                [q_ref[b, :, (kv * AT_GROUP + g) * AT_HEAD_DIM:(kv * AT_GROUP + g + 1) * AT_HEAD_DIM]
                 for g in range(AT_GROUP)], axis=0).astype(BF16)
            lw = lax.dot_general(qs, kall, NT_DIMS, preferred_element_type=F32) - slope * distf
            lw = jnp.where(valid, lw, -jnp.inf)
            lm = lax.dot_general(qs, kmeta, NT_DIMS, preferred_element_type=F32)
            m = jnp.maximum(jnp.maximum(jnp.max(lw, axis=-1, keepdims=True),
                                        jnp.max(lm, axis=-1, keepdims=True)), sink)
            ew = jnp.exp(lw - m)
            em = jnp.exp(lm - m)
            den = (jnp.sum(ew, axis=-1, keepdims=True) + jnp.sum(em, axis=-1, keepdims=True)
                   + jnp.exp(sink - m))
            o = (jnp.dot(ew.astype(BF16), vall, preferred_element_type=F32)
                 + jnp.dot(em.astype(BF16), vmeta, preferred_element_type=F32)) / den
            for g in range(AT_GROUP):
                h = kv * AT_GROUP + g
                o_ref[b, :, h * AT_HEAD_DIM:(h + 1) * AT_HEAD_DIM] = o[g * ds:(g + 1) * ds]


def _swa_sample(sinks, aq, ak, av, win_k, win_v, meta_k, meta_v, bg):
    b, ds = aq.shape[:2]
    spec = lambda a: pl.BlockSpec((bg,) + a.shape[1:], lambda i: (i, 0, 0))
    args = (aq, ak, av, win_k, win_v, meta_k, meta_v)
    return pl.pallas_call(
        functools.partial(_swa_sample_kernel, bg=bg, ds=ds),
        grid=(b // bg,),
        in_specs=[pl.BlockSpec(memory_space=pltpu.SMEM)] + [spec(a) for a in args],
        out_specs=spec(aq),
        out_shape=jax.ShapeDtypeStruct(aq.shape, F32),
        compiler_params=_cparams("parallel"),
        name="swa_sample",
    )(sinks, *args)


def _merge_kernel(x_ref, oa_ref, ob_ref, sga_ref, sgb_ref, wpa_ref, wpb_ref, wo_ref, gain_ref, wq_ref, sk_ref,
                  h2_ref, xn_ref, sc_ref):
    pa = jnp.dot(oa_ref[...].astype(BF16), wpa_ref[...], preferred_element_type=F32)
    pb = jnp.dot(ob_ref[...].astype(BF16), wpb_ref[...], preferred_element_type=F32)
    m = sga_ref[...] * pa + sgb_ref[...] * pb
    h2 = x_ref[...] + jnp.dot(m.astype(BF16), wo_ref[...], preferred_element_type=F32)
    h2_ref[...] = h2
    xn = h2 * lax.rsqrt(jnp.mean(h2 * h2, axis=-1, keepdims=True) + EPS) * gain_ref[...]
    xn_ref[...] = xn
    qp = jnp.dot(xn.astype(BF16), wq_ref[...], preferred_element_type=F32).astype(BF16)
    for hc in range(2 * PEER_HEADS):
        sc_ref[hc] = lax.dot_general(sk_ref[hc], qp[:, hc * PEER_HALF:(hc + 1) * PEER_HALF], NT_DIMS,
                                     preferred_element_type=F32)


def _merge(x, oa, ob, sga, sgb, wpa, wpb, wo, gain, wq, sk, tm):
    n = x.shape[0]
    row = lambda w: pl.BlockSpec((tm, w), lambda i: (i, 0))
    full = lambda a: pl.BlockSpec(a.shape, lambda i: (0,) * a.ndim)
    return pl.pallas_call(
        _merge_kernel,
        grid=(n // tm,),
        in_specs=[row(D_MODEL), row(HG_WIDTH), row(AT_WIDTH), row(D_MODEL), row(D_MODEL),
                  full(wpa), full(wpb), full(wo), full(gain), full(wq), full(sk)],
        out_specs=[row(D_MODEL), row(D_MODEL),
                   pl.BlockSpec((2 * PEER_HEADS, PEER_NKEYS, tm), lambda i: (0, 0, i))],
        out_shape=[jax.ShapeDtypeStruct((n, D_MODEL), F32), jax.ShapeDtypeStruct((n, D_MODEL), F32),
                   jax.ShapeDtypeStruct((2 * PEER_HEADS, PEER_NKEYS, n), F32)],
        compiler_params=_cparams("parallel"),
        name="merge",
    )(x, oa, ob, sga, sgb, wpa, wpb, wo, gain, wq, sk)


def _top16(arrays):
    arrays = list(arrays)
    ios = [lax.broadcasted_iota(jnp.int32, s.shape, 0) for s in arrays]
    vals = [[] for _ in arrays]
    idxs = [[] for _ in arrays]
    for _ in range(PEER_TOPK):
        for k, (s, io) in enumerate(zip(arrays, ios)):
            m = jnp.max(s, axis=0, keepdims=True)
            idx = jnp.min(jnp.where(s == m, io, s.shape[0]), axis=0, keepdims=True)
            vals[k].append(m)
            idxs[k].append(idx)
            arrays[k] = jnp.where(io == idx, -jnp.inf, s)
    return vals, idxs


_CAND_A0, _CAND_MID, _CAND_ROWS = PEER_TOPK, PEER_TOPK + 7 * SUBLANES, PEER_TOPK + 8 * SUBLANES


def _candidates(v1, v2):
    t = v1[0].shape[1]
    io8 = lax.broadcasted_iota(jnp.int32, (SUBLANES, t), 0)
    v1s = jnp.concatenate(v1, axis=0)
    v2s = jnp.concatenate(v2, axis=0)
    pieces = [v1[0] + v2s]
    for a in range(1, SUBLANES):
        keep = PEER_TOPK // (a + 1)
        p = v1[a] + v2s[0:SUBLANES]
        pieces.append(p if keep >= SUBLANES else jnp.where(io8 < keep, p, -jnp.inf))
    pieces.append(v1s[SUBLANES:] + v2[0])
    return jnp.concatenate(pieces, axis=0)


def _candidate_ab(pos):
    mid = pos - _CAND_A0
    a = jnp.where(pos < _CAND_A0, 0, jnp.where(pos < _CAND_MID, (mid >> 3) + 1, pos - (_CAND_MID - SUBLANES)))
    b = jnp.where(pos < _CAND_A0, pos, jnp.where(pos < _CAND_MID, mid & (SUBLANES - 1), 0))
    return a, b


def _topk_kernel(s_ref, eidx_ref, gate_ref):
    io16 = lax.broadcasted_iota(jnp.int32, (PEER_TOPK, s_ref.shape[2]), 0)
    for h0 in range(0, PEER_HEADS, 2):
        heads = (h0, h0 + 1)
        sub = [_top16([s_ref[2 * h], s_ref[2 * h + 1]]) for h in heads]
        ctops, cposs = _top16([_candidates(vals[0], vals[1]) for vals, _ in sub])
        for h, (_, idxs), ctop, cpos in zip(heads, sub, ctops, cposs):
            i1s = jnp.concatenate(idxs[0], axis=0)
            i2s = jnp.concatenate(idxs[1], axis=0)
            es = []
            for j in range(PEER_TOPK):
                a, b = _candidate_ab(cpos[j])
                e1 = jnp.sum(jnp.where(io16 == a, i1s, 0), axis=0, keepdims=True)
                e2 = jnp.sum(jnp.where(io16 == b, i2s, 0), axis=0, keepdims=True)
                es.append(e1 * PEER_NKEYS + e2)
            ex = jnp.exp(jnp.concatenate(ctop, axis=0) - ctop[0])
            sl = slice(h * PEER_TOPK, (h + 1) * PEER_TOPK)
            gate_ref[sl, :] = ex / jnp.sum(ex, axis=0, keepdims=True)
            eidx_ref[sl, :] = jnp.concatenate(es, axis=0)


def _topk(scores_t, tk):
    n = scores_t.shape[2]
    return pl.pallas_call(
        _topk_kernel,
        grid=(n // tk,),
        in_specs=[pl.BlockSpec((2 * PEER_HEADS, PEER_NKEYS, tk), lambda i: (0, 0, i))],
        out_specs=[pl.BlockSpec((PEER_PICKS, tk), lambda i: (0, i))] * 2,
        out_shape=[jax.ShapeDtypeStruct((PEER_PICKS, n), jnp.int32),
                   jax.ShapeDtypeStruct((PEER_PICKS, n), F32)],
        compiler_params=_cparams("parallel"),
        name="topk",
    )(scores_t)


def _table_rows(tab):
    return tab.astype(BF16).reshape(tab.shape[0], SUBLANES, LANES)


_BITREV = tuple(int(format(i, "03b")[::-1], 2) for i in range(SUBLANES))


def _fold_sublanes(a, b, half):
    s = lax.broadcasted_iota(jnp.int32, (SUBLANES, LANES), 0)
    low = (s & half) == 0
    if 2 * half < SUBLANES:
        b = pltpu.roll(b, half, axis=0)
    return jnp.where(low, a, b) + pltpu.roll(jnp.where(low, b, a), SUBLANES - half, axis=0)


def _split_indices(eidx_t):
    n = eidx_t.shape[1]
    return eidx_t.reshape(PEER_OCTETS, SUBLANES, n).transpose(1, 2, 0).reshape(SUBLANES, n * PEER_OCTETS)


def _for_each_index_slot(idx_hbm, sm_refs, sem, tb, process):
    step = pl.program_id(0)
    words = tb * PEER_OCTETS

    def copies(block, slot):
        return [pltpu.make_async_copy(idx_hbm.at[i, pl.ds(block * words, words)], sm_refs[slot][i], sem.at[slot, i])
                for i in range(SUBLANES)]

    def start(block, slot):
        for c in copies(block, slot):
            c.start()

    @pl.when(step == 0)
    def _():
        start(0, 0)

    for slot in range(PEER_SLOTS):
        block = step * PEER_SLOTS + slot
        for c in copies(block, slot):
            c.wait()
        if slot + 1 < PEER_SLOTS:
            start(block + 1, slot + 1)
        else:
            @pl.when(step + 1 < pl.num_programs(0))
            def _():
                start(block + 1, 0)
        process(slot)


def _index_scratch(tb):
    return ([pltpu.SMEM((tb * PEER_OCTETS,), jnp.int32) for _ in range(PEER_SLOTS * SUBLANES)]
            + [pltpu.SemaphoreType.DMA((PEER_SLOTS, SUBLANES))])


def _index_refs(refs):
    return [refs[s * SUBLANES:(s + 1) * SUBLANES] for s in range(PEER_SLOTS)]


def _peer_a_kernel(idx_hbm, x_ref, gate_ref, tab_ref, w_ref, *scratch, tb):
    sm_refs, sem, q_ref = _index_refs(scratch), scratch[-2], scratch[-1]
    lane = lax.broadcasted_iota(jnp.int32, (PEER_PICKS, tb), 1)

    def process(slot):
        t0 = slot * tb
        sm = sm_refs[slot]

        def tok(t, carry):
            xv = x_ref[t0 + t]

            def group(g, c2):
                for sub in range(PEER_UNROLL // SUBLANES):
                    octet = g * (PEER_UNROLL // SUBLANES) + sub
                    r0 = octet * SUBLANES
                    base = t * PEER_OCTETS + octet
                    tiles = [tab_ref[sm[_BITREV[i]][base]].astype(F32) * xv for i in range(SUBLANES)]
                    half = SUBLANES // 2
                    while len(tiles) > 1:
                        tiles = [_fold_sublanes(tiles[i], tiles[i + 1], half) for i in range(0, len(tiles), 2)]
                        half //= 2
                    q_ref[t, pl.ds(pl.multiple_of(r0, SUBLANES), SUBLANES), :] = tiles[0]
                return c2

            lax.fori_loop(0, PEER_PICKS // PEER_UNROLL, group, 0)
            return carry

        lax.fori_loop(0, tb, tok, 0)

        def reduce(tg, h):
            for i in range(SUBLANES):
                t = tg * SUBLANES + i
                h = jnp.where(lane == t, jnp.sum(q_ref[t], axis=1, keepdims=True), h)
            return h

        h = lax.fori_loop(0, tb // SUBLANES, reduce, jnp.zeros((PEER_PICKS, tb), F32))
        w_ref[:, t0:t0 + tb] = gate_ref[:, t0:t0 + tb] * (0.5 * h * (1.0 + lax.erf(h * math.sqrt(0.5))))

    _for_each_index_slot(idx_hbm, sm_refs, sem, tb, process)


def _peer_a(eidx_split, xn3, gate_t, tab, tb):
    n = xn3.shape[0]
    span = PEER_SLOTS * tb
    return pl.pallas_call(
        functools.partial(_peer_a_kernel, tb=tb),
        grid=(n // span,),
        in_specs=[pl.BlockSpec(memory_space=pl.ANY),
                  pl.BlockSpec((span, SUBLANES, LANES), lambda i: (i, 0, 0)),
                  pl.BlockSpec((PEER_PICKS, span), lambda i: (0, i)),
                  pl.BlockSpec(tab.shape, lambda i: (0, 0, 0), pipeline_mode=pl.Buffered(1))],
        out_specs=pl.BlockSpec((PEER_PICKS, span), lambda i: (0, i)),
        out_shape=jax.ShapeDtypeStruct((PEER_PICKS, n), F32),
        scratch_shapes=_index_scratch(tb) + [pltpu.VMEM((tb, PEER_PICKS, LANES), F32)],
        compiler_params=_cparams("arbitrary"),
        name="peer_a",
    )(eidx_split, xn3, gate_t, tab)


def _peer_b_kernel(idx_hbm, wt_ref, h2_ref, gain_ref, tab_ref, y_ref, *scratch, tb):
    sm_refs, sem, wb_ref = _index_refs(scratch), scratch[-2], scratch[-1]
    nacc = 4

    def process(slot):
        t0 = slot * tb
        sm = sm_refs[slot]
        wt = wt_ref[:, t0:t0 + tb]

        def spread(tg, carry):
            for i in range(SUBLANES):
                t = tg * SUBLANES + i
                wb_ref[t] = jnp.take_along_axis(wt, jnp.full((PEER_PICKS, LANES), t, jnp.int32), axis=1)
            return carry

        lax.fori_loop(0, tb // SUBLANES, spread, 0)

        def tok(t, carry):
            def group(g, acc):
                acc = list(acc)
                for sub in range(PEER_UNROLL_B // SUBLANES):
                    octet = g * (PEER_UNROLL_B // SUBLANES) + sub
                    base = t * PEER_OCTETS + octet
                    wg = wb_ref[t, pl.ds(pl.multiple_of(octet * SUBLANES, SUBLANES), SUBLANES), :]
                    for i in range(SUBLANES):
                        tile = tab_ref[sm[i][base]].astype(F32)
                        acc[i % nacc] = acc[i % nacc] + wg[i:i + 1, :] * tile
                return tuple(acc)

            acc = lax.fori_loop(0, PEER_PICKS // PEER_UNROLL_B, group,
                                tuple(jnp.zeros((SUBLANES, LANES), F32) for _ in range(nacc)))
            y_ref[t0 + t] = h2_ref[t0 + t] + ((acc[0] + acc[1]) + (acc[2] + acc[3]))
            return carry

        lax.fori_loop(0, tb, tok, 0)

    _for_each_index_slot(idx_hbm, sm_refs, sem, tb, process)

    y = y_ref[...]
    ms = jnp.sum(jnp.sum(y * y, axis=2, keepdims=True), axis=1, keepdims=True) * (1.0 / D_MODEL)
    y_ref[...] = y * lax.rsqrt(ms + EPS) * gain_ref[...][None]


def _peer_b(eidx_split, w_t, h23, gain3, tab, tb):
    n = h23.shape[0]
    span = PEER_SLOTS * tb
    tok3 = pl.BlockSpec((span, SUBLANES, LANES), lambda i: (i, 0, 0))
    return pl.pallas_call(
        functools.partial(_peer_b_kernel, tb=tb),
        grid=(n // span,),
        in_specs=[pl.BlockSpec(memory_space=pl.ANY),
                  pl.BlockSpec((PEER_PICKS, span), lambda i: (0, i)),
                  tok3, pl.BlockSpec(gain3.shape, lambda i: (0, 0)),
                  pl.BlockSpec(tab.shape, lambda i: (0, 0, 0), pipeline_mode=pl.Buffered(1))],
        out_specs=tok3,
        out_shape=jax.ShapeDtypeStruct(h23.shape, F32),
        scratch_shapes=_index_scratch(tb) + [pltpu.VMEM((tb, PEER_PICKS, LANES), F32)],
        compiler_params=_cparams("arbitrary"),
        name="peer_b",
    )(eidx_split, w_t, h23, gain3, tab)


def _mix_and_ffn(x, oa, ob, sga, sgb, wts):
    n = x.shape[0]
    tm, tk, tb = _tile(n, 256), _tile(n, LANES), LANES
    assert n % (PEER_SLOTS * tb) == 0, "PEER kernels take whole groups of token blocks"
    h2, xn, scores_t = _merge(x, oa, ob, sga, sgb, wts["wpa"], wts["wpb"], wts["wo"], wts["gain_ffn"],
                              wts["wq"], wts["sk"], tm)
    eidx_t, gate_t = _topk(scores_t, tk)
    eidx_split = _split_indices(eidx_t)
    w_t = _peer_a(eidx_split, xn.reshape(n, SUBLANES, LANES), gate_t, wts["u"], tb)
    y3 = _peer_b(eidx_split, w_t, h2.reshape(n, SUBLANES, LANES), wts["gain_final"], wts["v"], tb)
    return y3.reshape(n, D_MODEL)


def kernel(x_prompt, x_sample, cache_win_k, cache_win_v, cache_meta_k, cache_meta_v, state_hgrn, meta_tokens,
           norm_mix, w_in, hg_lb, hg_norm, attn_sinks, w_pa, w_pb, w_o, norm_ffn, peer_wq, peer_subkeys,
           peer_u, peer_v, norm_final):
    depth = w_in.shape[0]
    assert depth == 1, "single-layer step only"
    b, seq, _ = x_prompt.shape
    db, ds, _ = x_sample.shape
    l = 0
    lb = jnp.cumsum(jax.nn.softmax(hg_lb.astype(F32), axis=0), axis=0)[l][None, :]
    gain_mix = norm_mix[l][None, :]
    w_in_b = w_in[l].astype(BF16)
    hg_gain = hg_norm[l][None, :]
    sinks = attn_sinks[l].astype(F32)
    wts = dict(
        wpa=w_pa[l].astype(BF16), wpb=w_pb[l].astype(BF16), wo=w_o[l].astype(BF16),
        gain_ffn=norm_ffn[l][None, :], wq=peer_wq[l].astype(BF16),
        sk=peer_subkeys[l].reshape(2 * PEER_HEADS, PEER_NKEYS, PEER_HALF).astype(BF16),
        u=_table_rows(peer_u[l]), v=_table_rows(peer_v[l]),
        gain_final=norm_final.reshape(SUBLANES, LANES),
    )

    mq, mk, mlf, mv, msg, _, km, vm, _, _ = _proj(meta_tokens, gain_mix, lb, w_in_b, N_META)
    zero_state = jnp.zeros((1, HG_HEADS, HG_DK, HG_DK), F32)
    _, s_meta = _hgrn(mq, mk, mlf, mv, msg, hg_gain, zero_state, 1, N_META, N_META)

    xp = x_prompt.reshape(b * seq, D_MODEL)
    q, k, lf, v, sg, aq, ak, av, sga, sgb = _proj(xp, gain_mix, lb, w_in_b, _tile(b * seq, 256))
    oa, s_fin = _hgrn(q, k, lf, v, sg, hg_gain, s_meta, b, HG_CHUNK, _tile(seq, 512))
    ob = _swa_prompt(sinks, aq, ak, av, km, vm, b)
    y_prompt = _mix_and_ffn(xp, oa, ob, sga, sgb, wts).reshape(b, seq, D_MODEL)
    akr = ak.reshape(b, seq, AT_KV_HEADS, AT_HEAD_DIM)
    avr = av.reshape(b, seq, AT_KV_HEADS, AT_HEAD_DIM)
    kmr = jnp.broadcast_to(km.reshape(1, N_META, AT_KV_HEADS, AT_HEAD_DIM), (b, N_META, AT_KV_HEADS, AT_HEAD_DIM))
    vmr = jnp.broadcast_to(vm.reshape(1, N_META, AT_KV_HEADS, AT_HEAD_DIM), (b, N_META, AT_KV_HEADS, AT_HEAD_DIM))

    xs = x_sample.reshape(db * ds, D_MODEL)
    q, k, lf, v, sg, aq, aks, avs, sga, sgb = _proj(xs, gain_mix, lb, w_in_b, _tile(db * ds, 256))
    cs = SUBLANES
    pad = lambda a: jnp.pad(a.reshape(db, ds, HG_WIDTH), ((0, 0), (0, cs - ds), (0, 0))).reshape(db * cs, HG_WIDTH)
    oa, s_new = _hgrn(pad(q), pad(k), pad(lf), pad(v), pad(sg), hg_gain, state_hgrn[l].astype(F32), db, cs, cs)
    oa = oa.reshape(db, cs, HG_WIDTH)[:, :ds].reshape(db * ds, HG_WIDTH)
    ob = _swa_sample(sinks, aq.reshape(db, ds, AT_WIDTH), aks.reshape(db, ds, AT_KV_WIDTH),
                     avs.reshape(db, ds, AT_KV_WIDTH),
                     cache_win_k[l].reshape(db, WINDOW, AT_KV_WIDTH), cache_win_v[l].reshape(db, WINDOW, AT_KV_WIDTH),
                     cache_meta_k[l].reshape(db, N_META, AT_KV_WIDTH), cache_meta_v[l].reshape(db, N_META, AT_KV_WIDTH),
                     8).reshape(db * ds, AT_WIDTH)
    y_sample = _mix_and_ffn(xs, oa, ob, sga, sgb, wts).reshape(db, ds, D_MODEL)

    return (y_prompt, y_sample,
            akr[:, -WINDOW:][None], avr[:, -WINDOW:][None], kmr[None], vmr[None], s_fin[None],
            aks.reshape(1, db, ds, AT_KV_HEADS, AT_HEAD_DIM), avs.reshape(1, db, ds, AT_KV_HEADS, AT_HEAD_DIM),
            s_new[None])
```

```python
import functools
import math

import numpy as np
import jax
import jax.numpy as jnp
from jax import lax
from jax.experimental import pallas as pl
from jax.experimental.pallas import tpu as pltpu

F32 = jnp.float32
BF16 = jnp.bfloat16
EPS = 1e-6
D_MODEL = 1024
N_META = 16
HG_HEADS = 4
HG_DK = 128
HG_WIDTH = HG_HEADS * HG_DK
HG_CHUNK = 64
AT_HEADS = 8
AT_KV_HEADS = 2
AT_GROUP = AT_HEADS // AT_KV_HEADS
AT_HEAD_DIM = 64
AT_WIDTH = AT_HEADS * AT_HEAD_DIM
AT_KV_WIDTH = AT_KV_HEADS * AT_HEAD_DIM
AT_SCALE = AT_HEAD_DIM ** -0.5
WINDOW = 128
BLOCK = 128
PAST_LEN = 16384
PEER_HEADS = 8
PEER_NKEYS = 128
PEER_HALF = 128
PEER_TOPK = 16
PEER_PICKS = PEER_HEADS * PEER_TOPK
PEER_OCTETS = PEER_PICKS // 8
SUBLANES = 8
LANES = 128
PEER_UNROLL = 128
PEER_UNROLL_B = 32
PEER_SLOTS = 2
VMEM_LIMIT = 56 * 1024 * 1024

NT_DIMS = (((1,), (1,)), ((), ()))
TN_DIMS = (((0,), (0,)), ((), ()))


def _cparams(*sem):
    return pltpu.CompilerParams(dimension_semantics=sem, vmem_limit_bytes=VMEM_LIMIT)


def _tile(n, pref):
    return pref if n % pref == 0 else n


def _sigmoid(x):
    return 1.0 / (1.0 + jnp.exp(-x))


def _proj_kernel(x_ref, gain_ref, lb_ref, w_ref, q_ref, k_ref, lf_ref, v_ref, sg_ref,
                 aq_ref, ak_ref, av_ref, sga_ref, sgb_ref):
    x = x_ref[...]
    xn = x * lax.rsqrt(jnp.mean(x * x, axis=-1, keepdims=True) + EPS) * gain_ref[...]
    xb = xn.astype(BF16)

    def proj(a, b):
        return jnp.dot(xb, w_ref[:, a:b], preferred_element_type=F32)

    o = 0
    hq = proj(o, o + HG_WIDTH); o += HG_WIDTH
    q_ref[...] = hq * _sigmoid(hq)
    lb = lb_ref[...]
    f = lb + (1.0 - lb) * _sigmoid(proj(o, o + HG_WIDTH)); o += HG_WIDTH
    k_ref[...] = 1.0 - f
    lf_ref[...] = jnp.log(f)
    v_ref[...] = proj(o, o + HG_WIDTH); o += HG_WIDTH
    hg = proj(o, o + HG_WIDTH); o += HG_WIDTH
    sg_ref[...] = hg * _sigmoid(hg)
    aq_ref[...] = proj(o, o + AT_WIDTH) * AT_SCALE; o += AT_WIDTH
    ak_ref[...] = proj(o, o + AT_KV_WIDTH); o += AT_KV_WIDTH
    av_ref[...] = proj(o, o + AT_KV_WIDTH); o += AT_KV_WIDTH
    sga_ref[...] = _sigmoid(proj(o, o + D_MODEL)); o += D_MODEL
    sgb_ref[...] = _sigmoid(proj(o, o + D_MODEL))


def _proj(x, gain, lb, w_bf16, tm):
    n = x.shape[0]
    widths = (HG_WIDTH,) * 5 + (AT_WIDTH, AT_KV_WIDTH, AT_KV_WIDTH, D_MODEL, D_MODEL)
    row = lambda w: pl.BlockSpec((tm, w), lambda i: (i, 0))
    full = lambda a: pl.BlockSpec(a.shape, lambda i: (0,) * a.ndim)
    return pl.pallas_call(
        _proj_kernel,
        grid=(n // tm,),
        in_specs=[row(D_MODEL), full(gain), full(lb), full(w_bf16)],
        out_specs=[row(w) for w in widths],
        out_shape=[jax.ShapeDtypeStruct((n, w), F32) for w in widths],
        compiler_params=_cparams("parallel"),
        name="proj",
    )(x, gain, lb, w_bf16)


def _hgrn_consts(c):
    levels = int(round(math.log2(c)))
    t = np.arange(c)
    le = (t[None, :] <= t[:, None]).astype(np.float32)
    mats = [le]
    masks = []
    for l in range(levels):
        bs, half = 2 << l, 1 << l
        base = (t // bs) * bs
        bnd = base + half - 1
        mats.append(le - (t[None, :] <= bnd[:, None]).astype(np.float32))
        right = (t % bs) >= half
        masks.append((base[:, None] == base[None, :]) & right[:, None] & (~right)[None, :])
    masks.append(np.eye(c, dtype=bool))
    a = np.concatenate(mats, 0)
    a2 = np.concatenate([a, a], 1)
    return jnp.asarray(a2, BF16), jnp.asarray(np.stack(masks).astype(np.float32))


def _hgrn_kernel(a_ref, m_ref, q_ref, k_ref, lf_ref, v_ref, sg_ref, gain_ref, s0_ref, o_ref, sfin_ref, *st_ref,
                 c, nchunk):
    levels = int(round(math.log2(c)))
    i = pl.program_id(1)

    @pl.when(i == 0)
    def _():
        for h in range(HG_HEADS):
            st_ref[h][...] = s0_ref[0, h].T

    gain = gain_ref[...]

    def chunk(ci, carry):
        r0 = pl.multiple_of(ci * c, c)
        for h in range(HG_HEADS):
            sl = (pl.ds(r0, c), slice(h * HG_DK, (h + 1) * HG_DK))
            q = q_ref[sl]
            k = k_ref[sl]
            v = v_ref[sl]
            lf = lf_ref[sl]
            st = st_ref[h][...]
            hi = lf.astype(BF16)
            lo = (lf - hi.astype(F32)).astype(BF16)
            r = jnp.dot(a_ref[...], jnp.concatenate([hi, lo], axis=0), preferred_element_type=F32)
            cum = r[0:c]
            qb = q.astype(BF16)
            kb = k.astype(BF16)
            vb = v.astype(BF16)
            o = lax.dot_general((q * jnp.exp(cum)).astype(BF16), st.astype(BF16), NT_DIMS,
                                preferred_element_type=F32)
            att = m_ref[levels] * lax.dot_general(qb, kb, NT_DIMS, preferred_element_type=F32)
            for l in range(levels):
                e = r[(l + 1) * c:(l + 2) * c]
                aq = (q * jnp.exp(jnp.minimum(e, 0.0))).astype(BF16)
                ak = (k * jnp.exp(jnp.minimum(-e, 0.0))).astype(BF16)
                att = att + m_ref[l] * lax.dot_general(aq, ak, NT_DIMS, preferred_element_type=F32)
            o = o + jnp.dot(att.astype(BF16), vb, preferred_element_type=F32)
            last = cum[c - 1:c]
            kdec = (k * jnp.exp(last - cum)).astype(BF16)
            st_ref[h][...] = st * jnp.exp(last) + lax.dot_general(vb, kdec, TN_DIMS, preferred_element_type=F32)
            on = o * lax.rsqrt(jnp.mean(o * o, axis=-1, keepdims=True) + EPS) * gain
            o_ref[sl] = on * sg_ref[sl]
        return carry

    lax.fori_loop(0, nchunk, chunk, 0)

    @pl.when(i == pl.num_programs(1) - 1)
    def _():
        for h in range(HG_HEADS):
            sfin_ref[0, h] = st_ref[h][...].T


def _hgrn(q, k, lf, v, sg, gain, s0, batch, c, ct):
    n = q.shape[0]
    t = n // batch
    steps = t // ct
    a2, masks = _hgrn_consts(c)
    row = pl.BlockSpec((ct, HG_WIDTH), lambda b, i: (b * steps + i, 0))
    full = lambda a: pl.BlockSpec(a.shape, lambda b, i: (0,) * a.ndim)
    shared = s0.shape[0] == 1
    sspec = pl.BlockSpec((1, HG_HEADS, HG_DK, HG_DK), (lambda b, i: (0, 0, 0, 0)) if shared else (lambda b, i: (b, 0, 0, 0)))
    return pl.pallas_call(
        functools.partial(_hgrn_kernel, c=c, nchunk=ct // c),
        grid=(batch, steps),
        in_specs=[full(a2), full(masks), row, row, row, row, row, full(gain), sspec],
        out_specs=[row, pl.BlockSpec((1, HG_HEADS, HG_DK, HG_DK), lambda b, i: (b, 0, 0, 0))],
        out_shape=[jax.ShapeDtypeStruct((n, HG_WIDTH), F32),
                   jax.ShapeDtypeStruct((batch, HG_HEADS, HG_DK, HG_DK), F32)],
        scratch_shapes=[pltpu.VMEM((HG_DK, HG_DK), F32) for _ in range(HG_HEADS)],
        compiler_params=_cparams("parallel", "arbitrary"),
        name="hgrn",
    )(a2, masks, q, k, lf, v, sg, gain, s0)


def _swa_prompt_kernel(sink_ref, q_ref, kc_ref, kp_ref, vc_ref, vp_ref, km_ref, vm_ref, o_ref):
    n = pl.program_id(1)
    qi = lax.broadcasted_iota(jnp.int32, (BLOCK, 2 * BLOCK), 0)
    sj = lax.broadcasted_iota(jnp.int32, (BLOCK, 2 * BLOCK), 1)
    dist = qi + BLOCK - sj
    valid = (dist >= 0) & (dist <= WINDOW) & ((sj >= BLOCK) | (n > 0))
    distf = dist.astype(F32)
    for kv in range(AT_KV_HEADS):
        ks = slice(kv * AT_HEAD_DIM, (kv + 1) * AT_HEAD_DIM)
        kband = jnp.concatenate([kp_ref[:, ks], kc_ref[:, ks]], axis=0).astype(BF16)
        vband = jnp.concatenate([vp_ref[:, ks], vc_ref[:, ks]], axis=0).astype(BF16)
        kmeta = km_ref[:, ks].astype(BF16)
        vmeta = vm_ref[:, ks].astype(BF16)
        for g in range(AT_GROUP):
            h = kv * AT_GROUP + g
            hs = slice(h * AT_HEAD_DIM, (h + 1) * AT_HEAD_DIM)
            slope = 2.0 ** (-8.0 * (h + 1) / AT_HEADS)
            qh = q_ref[:, hs].astype(BF16)
            lband = lax.dot_general(qh, kband, NT_DIMS, preferred_element_type=F32) - slope * distf
            lband = jnp.where(valid, lband, -jnp.inf)
            lmeta = lax.dot_general(qh, kmeta, NT_DIMS, preferred_element_type=F32)
            sink = sink_ref[h]
            m = jnp.maximum(jnp.maximum(jnp.max(lband, axis=-1, keepdims=True),
                                        jnp.max(lmeta, axis=-1, keepdims=True)), sink)
            eb = jnp.exp(lband - m)
            em = jnp.exp(lmeta - m)
            den = (jnp.sum(eb, axis=-1, keepdims=True) + jnp.sum(em, axis=-1, keepdims=True)
                   + jnp.exp(sink - m))
            o = (jnp.dot(eb.astype(BF16), vband, preferred_element_type=F32)
                 + jnp.dot(em.astype(BF16), vmeta, preferred_element_type=F32))
            o_ref[:, hs] = o / den


def _swa_prompt(sinks, aq, ak, av, km, vm, batch):
    n = aq.shape[0]
    nb = n // batch // BLOCK
    cur = lambda w: pl.BlockSpec((BLOCK, w), lambda b, i: (b * nb + i, 0))
    prev = lambda w: pl.BlockSpec((BLOCK, w), lambda b, i: (b * nb + jnp.maximum(i - 1, 0), 0))
    meta = pl.BlockSpec((N_META, AT_KV_WIDTH), lambda b, i: (0, 0))
    return pl.pallas_call(
        _swa_prompt_kernel,
        grid=(batch, nb),
        in_specs=[pl.BlockSpec(memory_space=pltpu.SMEM), cur(AT_WIDTH), cur(AT_KV_WIDTH), prev(AT_KV_WIDTH),
                  cur(AT_KV_WIDTH), prev(AT_KV_WIDTH), meta, meta],
        out_specs=cur(AT_WIDTH),
        out_shape=jax.ShapeDtypeStruct((n, AT_WIDTH), F32),
        compiler_params=_cparams("parallel", "parallel"),
        name="swa_prompt",
    )(sinks, aq, ak, ak, av, av, km, vm)


def _swa_sample_kernel(sink_ref, q_ref, kn_ref, vn_ref, kw_ref, vw_ref, km_ref, vm_ref, o_ref, *, bg, ds):
    rows = AT_GROUP * ds
    nk = WINDOW + ds
    ri = lax.broadcasted_iota(jnp.int32, (rows, nk), 0)
    sj = lax.broadcasted_iota(jnp.int32, (rows, nk), 1)
    qpos = PAST_LEN + ri % ds
    kpos = PAST_LEN - WINDOW + sj
    dist = qpos - kpos
    valid = (dist >= 0) & (dist <= WINDOW) & (kpos >= N_META)
    distf = dist.astype(F32)
    gi = lax.broadcasted_iota(jnp.int32, (rows, 1), 0) // ds
    for kv in range(AT_KV_HEADS):
        ks = slice(kv * AT_HEAD_DIM, (kv + 1) * AT_HEAD_DIM)
        slope = jnp.exp2(-(gi + (kv * AT_GROUP + 1)).astype(F32) * (8.0 / AT_HEADS))
        sink = jnp.zeros((rows, 1), F32)
        for g in range(AT_GROUP):
            sink = jnp.where(gi == g, sink_ref[kv * AT_GROUP + g], sink)
        for b in range(bg):
            kall = jnp.concatenate([kw_ref[b, :, ks], kn_ref[b, :, ks]], axis=0).astype(BF16)
            vall = jnp.concatenate([vw_ref[b, :, ks], vn_ref[b, :, ks]], axis=0).astype(BF16)
            kmeta = km_ref[b, :, ks].astype(BF16)
            vmeta = vm_ref[b, :, ks].astype(BF16)
            qs = jnp.concatenate(
                [q_ref[b, :, (kv * AT_GROUP + g) * AT_HEAD_DIM:(kv * AT_GROUP + g + 1) * AT_HEAD_DIM]
                 for g in range(AT_GROUP)], axis=0).astype(BF16)
            lw = lax.dot_general(qs, kall, NT_DIMS, preferred_element_type=F32) - slope * distf
            lw = jnp.where(valid, lw, -jnp.inf)
            lm = lax.dot_general(qs, kmeta, NT_DIMS, preferred_element_type=F32)
            m = jnp.maximum(jnp.maximum(jnp.max(lw, axis=-1, keepdims=True),
                                        jnp.max(lm, axis=-1, keepdims=True)), sink)
            ew = jnp.exp(lw - m)
            em = jnp.exp(lm - m)
            den = (jnp.sum(ew, axis=-1, keepdims=True) + jnp.sum(em, axis=-1, keepdims=True)
                   + jnp.exp(sink - m))
            o = (jnp.dot(ew.astype(BF16), vall, preferred_element_type=F32)
                 + jnp.dot(em.astype(BF16), vmeta, preferred_element_type=F32)) / den
            for g in range(AT_GROUP):
                h = kv * AT_GROUP + g
                o_ref[b, :, h * AT_HEAD_DIM:(h + 1) * AT_HEAD_DIM] = o[g * ds:(g + 1) * ds]


def _swa_sample(sinks, aq, ak, av, win_k, win_v, meta_k, meta_v, bg):
    b, ds = aq.shape[:2]
    spec = lambda a: pl.BlockSpec((bg,) + a.shape[1:], lambda i: (i, 0, 0))
    args = (aq, ak, av, win_k, win_v, meta_k, meta_v)
    return pl.pallas_call(
        functools.partial(_swa_sample_kernel, bg=bg, ds=ds),
        grid=(b // bg,),
        in_specs=[pl.BlockSpec(memory_space=pltpu.SMEM)] + [spec(a) for a in args],
        out_specs=spec(aq),
        out_shape=jax.ShapeDtypeStruct(aq.shape, F32),
        compiler_params=_cparams("parallel"),
        name="swa_sample",
    )(sinks, *args)


def _merge_kernel(x_ref, oa_ref, ob_ref, sga_ref, sgb_ref, wpa_ref, wpb_ref, wo_ref, gain_ref, wq_ref, sk_ref,
                  h2_ref, xn_ref, sc_ref):
    pa = jnp.dot(oa_ref[...].astype(BF16), wpa_ref[...], preferred_element_type=F32)
    pb = jnp.dot(ob_ref[...].astype(BF16), wpb_ref[...], preferred_element_type=F32)
    m = sga_ref[...] * pa + sgb_ref[...] * pb
    h2 = x_ref[...] + jnp.dot(m.astype(BF16), wo_ref[...], preferred_element_type=F32)
    h2_ref[...] = h2
    xn = h2 * lax.rsqrt(jnp.mean(h2 * h2, axis=-1, keepdims=True) + EPS) * gain_ref[...]
    xn_ref[...] = xn
    qp = jnp.dot(xn.astype(BF16), wq_ref[...], preferred_element_type=F32).astype(BF16)
    for hc in range(2 * PEER_HEADS):
        sc_ref[hc] = lax.dot_general(sk_ref[hc], qp[:, hc * PEER_HALF:(hc + 1) * PEER_HALF], NT_DIMS,
                                     preferred_element_type=F32)


def _merge(x, oa, ob, sga, sgb, wpa, wpb, wo, gain, wq, sk, tm):
    n = x.shape[0]
    row = lambda w: pl.BlockSpec((tm, w), lambda i: (i, 0))
    full = lambda a: pl.BlockSpec(a.shape, lambda i: (0,) * a.ndim)
    return pl.pallas_call(
        _merge_kernel,
        grid=(n // tm,),
        in_specs=[row(D_MODEL), row(HG_WIDTH), row(AT_WIDTH), row(D_MODEL), row(D_MODEL),
                  full(wpa), full(wpb), full(wo), full(gain), full(wq), full(sk)],
        out_specs=[row(D_MODEL), row(D_MODEL),
                   pl.BlockSpec((2 * PEER_HEADS, PEER_NKEYS, tm), lambda i: (0, 0, i))],
        out_shape=[jax.ShapeDtypeStruct((n, D_MODEL), F32), jax.ShapeDtypeStruct((n, D_MODEL), F32),
                   jax.ShapeDtypeStruct((2 * PEER_HEADS, PEER_NKEYS, n), F32)],
        compiler_params=_cparams("parallel"),
        name="merge",
    )(x, oa, ob, sga, sgb, wpa, wpb, wo, gain, wq, sk)


def _top16(arrays):
    arrays = list(arrays)
    t = arrays[0].shape[1]
    io16 = lax.broadcasted_iota(jnp.int32, (PEER_TOPK, t), 0)
    ios = [lax.broadcasted_iota(jnp.int32, s.shape, 0).astype(F32) for s in arrays]
    vals = [jnp.zeros((PEER_TOPK, t), F32) for _ in arrays]
    idxs = [jnp.zeros((PEER_TOPK, t), F32) for _ in arrays]
    for j in range(PEER_TOPK):
        for k, (s, io) in enumerate(zip(arrays, ios)):
            m = jnp.max(s, axis=0, keepdims=True)
            idx = jnp.min(jnp.where(s == m, io, float(s.shape[0])), axis=0, keepdims=True)
            vals[k] = jnp.where(io16 == j, m, vals[k])
            idxs[k] = jnp.where(io16 == j, idx, idxs[k])
            arrays[k] = jnp.where(io == idx, -jnp.inf, s)
    return vals, [i.astype(jnp.int32) for i in idxs]


_CAND_A0, _CAND_MID, _CAND_ROWS = PEER_TOPK, PEER_TOPK + 7 * SUBLANES, PEER_TOPK + 8 * SUBLANES


def _candidates(v1, v2):
    t = v1.shape[1]
    io8 = lax.broadcasted_iota(jnp.int32, (SUBLANES, t), 0)
    pieces = [v1[0:1] + v2]
    for a in range(1, SUBLANES):
        keep = PEER_TOPK // (a + 1)
        p = v1[a:a + 1] + v2[0:SUBLANES]
        pieces.append(p if keep >= SUBLANES else jnp.where(io8 < keep, p, -jnp.inf))
    pieces.append(v1[SUBLANES:] + v2[0:1])
    return jnp.concatenate(pieces, axis=0)


def _candidate_ab(pos):
    mid = pos - _CAND_A0
    a = jnp.where(pos < _CAND_A0, 0, jnp.where(pos < _CAND_MID, (mid >> 3) + 1, pos - (_CAND_MID - SUBLANES)))
    b = jnp.where(pos < _CAND_A0, pos, jnp.where(pos < _CAND_MID, mid & (SUBLANES - 1), 0))
    return a, b


def _topk_kernel(s_ref, eidx_ref, gate_ref):
    for h0 in range(0, PEER_HEADS, 2):
        heads = (h0, h0 + 1)
        sub = [_top16([s_ref[2 * h], s_ref[2 * h + 1]]) for h in heads]
        ctops, cposs = _top16([_candidates(vals[0], vals[1]) for vals, _ in sub])
        for h, (_, (i1, i2)), ctop, cpos in zip(heads, sub, ctops, cposs):
            a, b = _candidate_ab(cpos)
            e1 = jnp.zeros_like(a)
            e2 = jnp.zeros_like(b)
            for r in range(PEER_TOPK):
                e1 = jnp.where(a == r, i1[r:r + 1], e1)
                e2 = jnp.where(b == r, i2[r:r + 1], e2)
            ex = jnp.exp(ctop - ctop[0:1])
            sl = slice(h * PEER_TOPK, (h + 1) * PEER_TOPK)
            gate_ref[sl, :] = ex / jnp.sum(ex, axis=0, keepdims=True)
            eidx_ref[sl, :] = e1 * PEER_NKEYS + e2


def _topk(scores_t, tk):
    n = scores_t.shape[2]
    return pl.pallas_call(
        _topk_kernel,
        grid=(n // tk,),
        in_specs=[pl.BlockSpec((2 * PEER_HEADS, PEER_NKEYS, tk), lambda i: (0, 0, i))],
        out_specs=[pl.BlockSpec((PEER_PICKS, tk), lambda i: (0, i))] * 2,
        out_shape=[jax.ShapeDtypeStruct((PEER_PICKS, n), jnp.int32),
                   jax.ShapeDtypeStruct((PEER_PICKS, n), F32)],
        compiler_params=_cparams("parallel"),
        name="topk",
    )(scores_t)


def _table_rows(tab):
    return tab.astype(BF16).reshape(tab.shape[0], SUBLANES, LANES)


_BITREV = tuple(int(format(i, "03b")[::-1], 2) for i in range(SUBLANES))


def _fold_sublanes(a, b, half):
    s = lax.broadcasted_iota(jnp.int32, (SUBLANES, LANES), 0)
    low = (s & half) == 0
    if 2 * half < SUBLANES:
        b = pltpu.roll(b, half, axis=0)
    return jnp.where(low, a, b) + pltpu.roll(jnp.where(low, b, a), SUBLANES - half, axis=0)


def _split_indices(eidx_t):
    n = eidx_t.shape[1]
    return eidx_t.reshape(PEER_OCTETS, SUBLANES, n).transpose(1, 2, 0).reshape(SUBLANES, n * PEER_OCTETS)


def _for_each_index_slot(idx_hbm, sm_refs, sem, tb, process):
    step = pl.program_id(0)
    words = tb * PEER_OCTETS

    def copies(block, slot):
        return [pltpu.make_async_copy(idx_hbm.at[i, pl.ds(block * words, words)], sm_refs[slot][i], sem.at[slot, i])
                for i in range(SUBLANES)]

    def start(block, slot):
        for c in copies(block, slot):
            c.start()

    @pl.when(step == 0)
    def _():
        start(0, 0)

    for slot in range(PEER_SLOTS):
        block = step * PEER_SLOTS + slot
        for c in copies(block, slot):
            c.wait()
        if slot + 1 < PEER_SLOTS:
            start(block + 1, slot + 1)
        else:
            @pl.when(step + 1 < pl.num_programs(0))
            def _():
                start(block + 1, 0)
        process(slot)


def _index_scratch(tb):
    return ([pltpu.SMEM((tb * PEER_OCTETS,), jnp.int32) for _ in range(PEER_SLOTS * SUBLANES)]
            + [pltpu.SemaphoreType.DMA((PEER_SLOTS, SUBLANES))])


def _index_refs(refs):
    return [refs[s * SUBLANES:(s + 1) * SUBLANES] for s in range(PEER_SLOTS)]


def _peer_a_kernel(idx_hbm, x_ref, gate_ref, tab_ref, w_ref, *scratch, tb):
    sm_refs, sem, q_ref = _index_refs(scratch), scratch[-2], scratch[-1]
    lane = lax.broadcasted_iota(jnp.int32, (PEER_PICKS, tb), 1)

    def process(slot):
        t0 = slot * tb
        sm = sm_refs[slot]

        def tok(t, carry):
            xv = x_ref[t0 + t]

            def group(g, c2):
                for sub in range(PEER_UNROLL // SUBLANES):
                    octet = g * (PEER_UNROLL // SUBLANES) + sub
                    r0 = octet * SUBLANES
                    base = t * PEER_OCTETS + octet
                    tiles = [tab_ref[sm[_BITREV[i]][base]].astype(F32) * xv for i in range(SUBLANES)]
                    half = SUBLANES // 2
                    while len(tiles) > 1:
                        tiles = [_fold_sublanes(tiles[i], tiles[i + 1], half) for i in range(0, len(tiles), 2)]
                        half //= 2
                    q_ref[t, pl.ds(pl.multiple_of(r0, SUBLANES), SUBLANES), :] = tiles[0]
                return c2

            lax.fori_loop(0, PEER_PICKS // PEER_UNROLL, group, 0)
            return carry

        lax.fori_loop(0, tb, tok, 0)

        def reduce(tg, h):
            for i in range(SUBLANES):
                t = tg * SUBLANES + i
                h = jnp.where(lane == t, jnp.sum(q_ref[t], axis=1, keepdims=True), h)
            return h

        h = lax.fori_loop(0, tb // SUBLANES, reduce, jnp.zeros((PEER_PICKS, tb), F32))
        w_ref[:, t0:t0 + tb] = gate_ref[:, t0:t0 + tb] * (0.5 * h * (1.0 + lax.erf(h * math.sqrt(0.5))))

    _for_each_index_slot(idx_hbm, sm_refs, sem, tb, process)


def _peer_a(eidx_split, xn3, gate_t, tab, tb):
    n = xn3.shape[0]
    span = PEER_SLOTS * tb
    return pl.pallas_call(
        functools.partial(_peer_a_kernel, tb=tb),
        grid=(n // span,),
        in_specs=[pl.BlockSpec(memory_space=pl.ANY),
                  pl.BlockSpec((span, SUBLANES, LANES), lambda i: (i, 0, 0)),
                  pl.BlockSpec((PEER_PICKS, span), lambda i: (0, i)),
                  pl.BlockSpec(tab.shape, lambda i: (0, 0, 0), pipeline_mode=pl.Buffered(1))],
        out_specs=pl.BlockSpec((PEER_PICKS, span), lambda i: (0, i)),
        out_shape=jax.ShapeDtypeStruct((PEER_PICKS, n), F32),
        scratch_shapes=_index_scratch(tb) + [pltpu.VMEM((tb, PEER_PICKS, LANES), F32)],
        compiler_params=_cparams("arbitrary"),
        name="peer_a",
    )(eidx_split, xn3, gate_t, tab)


def _peer_b_kernel(idx_hbm, wt_ref, h2_ref, gain_ref, tab_ref, y_ref, *scratch, tb):
    sm_refs, sem, wb_ref = _index_refs(scratch), scratch[-2], scratch[-1]
    nacc = 4

    def process(slot):
        t0 = slot * tb
        sm = sm_refs[slot]
        wt = wt_ref[:, t0:t0 + tb]

        def spread(tg, carry):
            for i in range(SUBLANES):
                t = tg * SUBLANES + i
                wb_ref[t] = jnp.take_along_axis(wt, jnp.full((PEER_PICKS, LANES), t, jnp.int32), axis=1)
            return carry

        lax.fori_loop(0, tb // SUBLANES, spread, 0)

        def tok(t, carry):
            def group(g, acc):
                acc = list(acc)
                for sub in range(PEER_UNROLL_B // SUBLANES):
                    octet = g * (PEER_UNROLL_B // SUBLANES) + sub
                    base = t * PEER_OCTETS + octet
                    wg = wb_ref[t, pl.ds(pl.multiple_of(octet * SUBLANES, SUBLANES), SUBLANES), :]
                    for i in range(SUBLANES):
                        tile = tab_ref[sm[i][base]].astype(F32)
                        acc[i % nacc] = acc[i % nacc] + wg[i:i + 1, :] * tile
                return tuple(acc)

            acc = lax.fori_loop(0, PEER_PICKS // PEER_UNROLL_B, group,
                                tuple(jnp.zeros((SUBLANES, LANES), F32) for _ in range(nacc)))
            y_ref[t0 + t] = h2_ref[t0 + t] + ((acc[0] + acc[1]) + (acc[2] + acc[3]))
            return carry

        lax.fori_loop(0, tb, tok, 0)

    _for_each_index_slot(idx_hbm, sm_refs, sem, tb, process)

    y = y_ref[...]
    ms = jnp.sum(jnp.sum(y * y, axis=2, keepdims=True), axis=1, keepdims=True) * (1.0 / D_MODEL)
    y_ref[...] = y * lax.rsqrt(ms + EPS) * gain_ref[...][None]


def _peer_b(eidx_split, w_t, h23, gain3, tab, tb):
    n = h23.shape[0]
    span = PEER_SLOTS * tb
    tok3 = pl.BlockSpec((span, SUBLANES, LANES), lambda i: (i, 0, 0))
    return pl.pallas_call(
        functools.partial(_peer_b_kernel, tb=tb),
        grid=(n // span,),
        in_specs=[pl.BlockSpec(memory_space=pl.ANY),
                  pl.BlockSpec((PEER_PICKS, span), lambda i: (0, i)),
                  tok3, pl.BlockSpec(gain3.shape, lambda i: (0, 0)),
                  pl.BlockSpec(tab.shape, lambda i: (0, 0, 0), pipeline_mode=pl.Buffered(1))],
        out_specs=tok3,
        out_shape=jax.ShapeDtypeStruct(h23.shape, F32),
        scratch_shapes=_index_scratch(tb) + [pltpu.VMEM((tb, PEER_PICKS, LANES), F32)],
        compiler_params=_cparams("arbitrary"),
        name="peer_b",
    )(eidx_split, w_t, h23, gain3, tab)


def _mix_and_ffn(x, oa, ob, sga, sgb, wts):
    n = x.shape[0]
    tm, tk, tb = _tile(n, 256), _tile(n, LANES), LANES
    assert n % (PEER_SLOTS * tb) == 0, "PEER kernels take whole groups of token blocks"
    h2, xn, scores_t = _merge(x, oa, ob, sga, sgb, wts["wpa"], wts["wpb"], wts["wo"], wts["gain_ffn"],
                              wts["wq"], wts["sk"], tm)
    eidx_t, gate_t = _topk(scores_t, tk)
    eidx_split = _split_indices(eidx_t)
    w_t = _peer_a(eidx_split, xn.reshape(n, SUBLANES, LANES), gate_t, wts["u"], tb)
    y3 = _peer_b(eidx_split, w_t, h2.reshape(n, SUBLANES, LANES), wts["gain_final"], wts["v"], tb)
    return y3.reshape(n, D_MODEL)


def kernel(x_prompt, x_sample, cache_win_k, cache_win_v, cache_meta_k, cache_meta_v, state_hgrn, meta_tokens,
           norm_mix, w_in, hg_lb, hg_norm, attn_sinks, w_pa, w_pb, w_o, norm_ffn, peer_wq, peer_subkeys,
           peer_u, peer_v, norm_final):
    depth = w_in.shape[0]
    assert depth == 1, "single-layer step only"
    b, seq, _ = x_prompt.shape
    db, ds, _ = x_sample.shape
    l = 0
    lb = jnp.cumsum(jax.nn.softmax(hg_lb.astype(F32), axis=0), axis=0)[l][None, :]
    gain_mix = norm_mix[l][None, :]
    w_in_b = w_in[l].astype(BF16)
    hg_gain = hg_norm[l][None, :]
    sinks = attn_sinks[l].astype(F32)
    wts = dict(
        wpa=w_pa[l].astype(BF16), wpb=w_pb[l].astype(BF16), wo=w_o[l].astype(BF16),
        gain_ffn=norm_ffn[l][None, :], wq=peer_wq[l].astype(BF16),
        sk=peer_subkeys[l].reshape(2 * PEER_HEADS, PEER_NKEYS, PEER_HALF).astype(BF16),
        u=_table_rows(peer_u[l]), v=_table_rows(peer_v[l]),
        gain_final=norm_final.reshape(SUBLANES, LANES),
    )

    mq, mk, mlf, mv, msg, _, km, vm, _, _ = _proj(meta_tokens, gain_mix, lb, w_in_b, N_META)
    zero_state = jnp.zeros((1, HG_HEADS, HG_DK, HG_DK), F32)
    _, s_meta = _hgrn(mq, mk, mlf, mv, msg, hg_gain, zero_state, 1, N_META, N_META)

    xp = x_prompt.reshape(b * seq, D_MODEL)
    q, k, lf, v, sg, aq, ak, av, sga, sgb = _proj(xp, gain_mix, lb, w_in_b, _tile(b * seq, 256))
    oa, s_fin = _hgrn(q, k, lf, v, sg, hg_gain, s_meta, b, HG_CHUNK, _tile(seq, 512))
    ob = _swa_prompt(sinks, aq, ak, av, km, vm, b)
    y_prompt = _mix_and_ffn(xp, oa, ob, sga, sgb, wts).reshape(b, seq, D_MODEL)
    akr = ak.reshape(b, seq, AT_KV_HEADS, AT_HEAD_DIM)
    avr = av.reshape(b, seq, AT_KV_HEADS, AT_HEAD_DIM)
    kmr = jnp.broadcast_to(km.reshape(1, N_META, AT_KV_HEADS, AT_HEAD_DIM), (b, N_META, AT_KV_HEADS, AT_HEAD_DIM))
    vmr = jnp.broadcast_to(vm.reshape(1, N_META, AT_KV_HEADS, AT_HEAD_DIM), (b, N_META, AT_KV_HEADS, AT_HEAD_DIM))

    xs = x_sample.reshape(db * ds, D_MODEL)
    q, k, lf, v, sg, aq, aks, avs, sga, sgb = _proj(xs, gain_mix, lb, w_in_b, _tile(db * ds, 256))
    cs = SUBLANES
    pad = lambda a: jnp.pad(a.reshape(db, ds, HG_WIDTH), ((0, 0), (0, cs - ds), (0, 0))).reshape(db * cs, HG_WIDTH)
    oa, s_new = _hgrn(pad(q), pad(k), pad(lf), pad(v), pad(sg), hg_gain, state_hgrn[l].astype(F32), db, cs, cs)
    oa = oa.reshape(db, cs, HG_WIDTH)[:, :ds].reshape(db * ds, HG_WIDTH)
    ob = _swa_sample(sinks, aq.reshape(db, ds, AT_WIDTH), aks.reshape(db, ds, AT_KV_WIDTH),
                     avs.reshape(db, ds, AT_KV_WIDTH),
                     cache_win_k[l].reshape(db, WINDOW, AT_KV_WIDTH), cache_win_v[l].reshape(db, WINDOW, AT_KV_WIDTH),
                     cache_meta_k[l].reshape(db, N_META, AT_KV_WIDTH), cache_meta_v[l].reshape(db, N_META, AT_KV_WIDTH),
                     8).reshape(db * ds, AT_WIDTH)
    y_sample = _mix_and_ffn(xs, oa, ob, sga, sgb, wts).reshape(db, ds, D_MODEL)

    return (y_prompt, y_sample,
            akr[:, -WINDOW:][None], avr[:, -WINDOW:][None], kmr[None], vmr[None], s_fin[None],
            aks.reshape(1, db, ds, AT_KV_HEADS, AT_HEAD_DIM), avs.reshape(1, db, ds, AT_KV_HEADS, AT_HEAD_DIM),
            s_new[None])
```

```python
import functools
import math

import numpy as np
import jax
import jax.numpy as jnp
from jax import lax
from jax.experimental import pallas as pl
from jax.experimental.pallas import tpu as pltpu

F32 = jnp.float32
BF16 = jnp.bfloat16
EPS = 1e-6
D_MODEL = 1024
N_META = 16
HG_HEADS = 4
HG_DK = 128
HG_WIDTH = HG_HEADS * HG_DK
HG_CHUNK = 64
AT_HEADS = 8
AT_KV_HEADS = 2
AT_GROUP = AT_HEADS // AT_KV_HEADS
AT_HEAD_DIM = 64
AT_WIDTH = AT_HEADS * AT_HEAD_DIM
AT_KV_WIDTH = AT_KV_HEADS * AT_HEAD_DIM
AT_SCALE = AT_HEAD_DIM ** -0.5
WINDOW = 128
BLOCK = 128
PAST_LEN = 16384
PEER_HEADS = 8
PEER_NKEYS = 128
PEER_HALF = 128
PEER_TOPK = 16
PEER_PICKS = PEER_HEADS * PEER_TOPK
PEER_OCTETS = PEER_PICKS // 8
SUBLANES = 8
LANES = 128
PEER_SLOTS = 2
VMEM_LIMIT = 56 * 1024 * 1024

NT_DIMS = (((1,), (1,)), ((), ()))
TN_DIMS = (((0,), (0,)), ((), ()))


def _cparams(*sem):
    return pltpu.CompilerParams(dimension_semantics=sem, vmem_limit_bytes=VMEM_LIMIT)


def _tile(n, pref):
    return pref if n % pref == 0 else n


def _sigmoid(x):
    return 1.0 / (1.0 + jnp.exp(-x))


def _proj_kernel(x_ref, gain_ref, lb_ref, w_ref, q_ref, k_ref, lf_ref, v_ref, sg_ref,
                 aq_ref, ak_ref, av_ref, sga_ref, sgb_ref):
    x = x_ref[...]
    xn = x * lax.rsqrt(jnp.mean(x * x, axis=-1, keepdims=True) + EPS) * gain_ref[...]
    xb = xn.astype(BF16)

    def proj(a, b):
        return jnp.dot(xb, w_ref[:, a:b], preferred_element_type=F32)

    o = 0
    hq = proj(o, o + HG_WIDTH); o += HG_WIDTH
    q_ref[...] = hq * _sigmoid(hq)
    lb = lb_ref[...]
    f = lb + (1.0 - lb) * _sigmoid(proj(o, o + HG_WIDTH)); o += HG_WIDTH
    k_ref[...] = 1.0 - f
    lf_ref[...] = jnp.log(f)
    v_ref[...] = proj(o, o + HG_WIDTH); o += HG_WIDTH
    hg = proj(o, o + HG_WIDTH); o += HG_WIDTH
    sg_ref[...] = hg * _sigmoid(hg)
    aq_ref[...] = proj(o, o + AT_WIDTH) * AT_SCALE; o += AT_WIDTH
    ak_ref[...] = proj(o, o + AT_KV_WIDTH); o += AT_KV_WIDTH
    av_ref[...] = proj(o, o + AT_KV_WIDTH); o += AT_KV_WIDTH
    sga_ref[...] = _sigmoid(proj(o, o + D_MODEL)); o += D_MODEL
    sgb_ref[...] = _sigmoid(proj(o, o + D_MODEL))


def _proj(x, gain, lb, w_bf16, tm):
    n = x.shape[0]
    widths = (HG_WIDTH,) * 5 + (AT_WIDTH, AT_KV_WIDTH, AT_KV_WIDTH, D_MODEL, D_MODEL)
    row = lambda w: pl.BlockSpec((tm, w), lambda i: (i, 0))
    full = lambda a: pl.BlockSpec(a.shape, lambda i: (0,) * a.ndim)
    return pl.pallas_call(
        _proj_kernel,
        grid=(n // tm,),
        in_specs=[row(D_MODEL), full(gain), full(lb), full(w_bf16)],
        out_specs=[row(w) for w in widths],
        out_shape=[jax.ShapeDtypeStruct((n, w), F32) for w in widths],
        compiler_params=_cparams("parallel"),
        name="proj",
    )(x, gain, lb, w_bf16)


def _hgrn_consts(c):
    levels = int(round(math.log2(c)))
    t = np.arange(c)
    le = (t[None, :] <= t[:, None]).astype(np.float32)
    mats = [le]
    masks = []
    for l in range(levels):
        bs, half = 2 << l, 1 << l
        base = (t // bs) * bs
        bnd = base + half - 1
        mats.append(le - (t[None, :] <= bnd[:, None]).astype(np.float32))
        right = (t % bs) >= half
        masks.append((base[:, None] == base[None, :]) & right[:, None] & (~right)[None, :])
    masks.append(np.eye(c, dtype=bool))
    a = np.concatenate(mats, 0)
    a2 = np.concatenate([a, a], 1)
    return jnp.asarray(a2, BF16), jnp.asarray(np.stack(masks).astype(np.float32))


def _hgrn_kernel(a_ref, m_ref, q_ref, k_ref, lf_ref, v_ref, sg_ref, gain_ref, s0_ref, o_ref, sfin_ref, *st_ref,
                 c, nchunk):
    levels = int(round(math.log2(c)))
    i = pl.program_id(1)

    @pl.when(i == 0)
    def _():
        for h in range(HG_HEADS):
            st_ref[h][...] = s0_ref[0, h].T

    gain = gain_ref[...]

    def chunk(ci, carry):
        r0 = pl.multiple_of(ci * c, c)
        for h in range(HG_HEADS):
            sl = (pl.ds(r0, c), slice(h * HG_DK, (h + 1) * HG_DK))
            q = q_ref[sl]
            k = k_ref[sl]
            v = v_ref[sl]
            lf = lf_ref[sl]
            st = st_ref[h][...]
            hi = lf.astype(BF16)
            lo = (lf - hi.astype(F32)).astype(BF16)
            r = jnp.dot(a_ref[...], jnp.concatenate([hi, lo], axis=0), preferred_element_type=F32)
            cum = r[0:c]
            qb = q.astype(BF16)
            kb = k.astype(BF16)
            vb = v.astype(BF16)
            o = lax.dot_general((q * jnp.exp(cum)).astype(BF16), st.astype(BF16), NT_DIMS,
                                preferred_element_type=F32)
            att = m_ref[levels] * lax.dot_general(qb, kb, NT_DIMS, preferred_element_type=F32)
            for l in range(levels):
                e = r[(l + 1) * c:(l + 2) * c]
                aq = (q * jnp.exp(jnp.minimum(e, 0.0))).astype(BF16)
                ak = (k * jnp.exp(jnp.minimum(-e, 0.0))).astype(BF16)
                att = att + m_ref[l] * lax.dot_general(aq, ak, NT_DIMS, preferred_element_type=F32)
            o = o + jnp.dot(att.astype(BF16), vb, preferred_element_type=F32)
            last = cum[c - 1:c]
            kdec = (k * jnp.exp(last - cum)).astype(BF16)
            st_ref[h][...] = st * jnp.exp(last) + lax.dot_general(vb, kdec, TN_DIMS, preferred_element_type=F32)
            on = o * lax.rsqrt(jnp.mean(o * o, axis=-1, keepdims=True) + EPS) * gain
            o_ref[sl] = on * sg_ref[sl]
        return carry

    lax.fori_loop(0, nchunk, chunk, 0)

    @pl.when(i == pl.num_programs(1) - 1)
    def _():
        for h in range(HG_HEADS):
            sfin_ref[0, h] = st_ref[h][...].T


def _hgrn(q, k, lf, v, sg, gain, s0, batch, c, ct):
    n = q.shape[0]
    t = n // batch
    steps = t // ct
    a2, masks = _hgrn_consts(c)
    row = pl.BlockSpec((ct, HG_WIDTH), lambda b, i: (b * steps + i, 0))
    full = lambda a: pl.BlockSpec(a.shape, lambda b, i: (0,) * a.ndim)
    shared = s0.shape[0] == 1
    sspec = pl.BlockSpec((1, HG_HEADS, HG_DK, HG_DK), (lambda b, i: (0, 0, 0, 0)) if shared else (lambda b, i: (b, 0, 0, 0)))
    return pl.pallas_call(
        functools.partial(_hgrn_kernel, c=c, nchunk=ct // c),
        grid=(batch, steps),
        in_specs=[full(a2), full(masks), row, row, row, row, row, full(gain), sspec],
        out_specs=[row, pl.BlockSpec((1, HG_HEADS, HG_DK, HG_DK), lambda b, i: (b, 0, 0, 0))],
        out_shape=[jax.ShapeDtypeStruct((n, HG_WIDTH), F32),
                   jax.ShapeDtypeStruct((batch, HG_HEADS, HG_DK, HG_DK), F32)],
        scratch_shapes=[pltpu.VMEM((HG_DK, HG_DK), F32) for _ in range(HG_HEADS)],
        compiler_params=_cparams("parallel", "arbitrary"),
        name="hgrn",
    )(a2, masks, q, k, lf, v, sg, gain, s0)


def _swa_prompt_kernel(sink_ref, q_ref, kc_ref, kp_ref, vc_ref, vp_ref, km_ref, vm_ref, o_ref):
    n = pl.program_id(1)
    qi = lax.broadcasted_iota(jnp.int32, (BLOCK, 2 * BLOCK), 0)
    sj = lax.broadcasted_iota(jnp.int32, (BLOCK, 2 * BLOCK), 1)
    dist = qi + BLOCK - sj
    valid = (dist >= 0) & (dist <= WINDOW) & ((sj >= BLOCK) | (n > 0))
    distf = dist.astype(F32)
    for kv in range(AT_KV_HEADS):
        ks = slice(kv * AT_HEAD_DIM, (kv + 1) * AT_HEAD_DIM)
        kband = jnp.concatenate([kp_ref[:, ks], kc_ref[:, ks]], axis=0).astype(BF16)
        vband = jnp.concatenate([vp_ref[:, ks], vc_ref[:, ks]], axis=0).astype(BF16)
        kmeta = km_ref[:, ks].astype(BF16)
        vmeta = vm_ref[:, ks].astype(BF16)
        for g in range(AT_GROUP):
            h = kv * AT_GROUP + g
            hs = slice(h * AT_HEAD_DIM, (h + 1) * AT_HEAD_DIM)
            slope = 2.0 ** (-8.0 * (h + 1) / AT_HEADS)
            qh = q_ref[:, hs].astype(BF16)
            lband = lax.dot_general(qh, kband, NT_DIMS, preferred_element_type=F32) - slope * distf
            lband = jnp.where(valid, lband, -jnp.inf)
            lmeta = lax.dot_general(qh, kmeta, NT_DIMS, preferred_element_type=F32)
            sink = sink_ref[h]
            m = jnp.maximum(jnp.maximum(jnp.max(lband, axis=-1, keepdims=True),
                                        jnp.max(lmeta, axis=-1, keepdims=True)), sink)
            eb = jnp.exp(lband - m)
            em = jnp.exp(lmeta - m)
            den = (jnp.sum(eb, axis=-1, keepdims=True) + jnp.sum(em, axis=-1, keepdims=True)
                   + jnp.exp(sink - m))
            o = (jnp.dot(eb.astype(BF16), vband, preferred_element_type=F32)
                 + jnp.dot(em.astype(BF16), vmeta, preferred_element_type=F32))
            o_ref[:, hs] = o / den


def _swa_prompt(sinks, aq, ak, av, km, vm, batch):
    n = aq.shape[0]
    nb = n // batch // BLOCK
    cur = lambda w: pl.BlockSpec((BLOCK, w), lambda b, i: (b * nb + i, 0))
    prev = lambda w: pl.BlockSpec((BLOCK, w), lambda b, i: (b * nb + jnp.maximum(i - 1, 0), 0))
    meta = pl.BlockSpec((N_META, AT_KV_WIDTH), lambda b, i: (0, 0))
    return pl.pallas_call(
        _swa_prompt_kernel,
        grid=(batch, nb),
        in_specs=[pl.BlockSpec(memory_space=pltpu.SMEM), cur(AT_WIDTH), cur(AT_KV_WIDTH), prev(AT_KV_WIDTH),
                  cur(AT_KV_WIDTH), prev(AT_KV_WIDTH), meta, meta],
        out_specs=cur(AT_WIDTH),
        out_shape=jax.ShapeDtypeStruct((n, AT_WIDTH), F32),
        compiler_params=_cparams("parallel", "parallel"),
        name="swa_prompt",
    )(sinks, aq, ak, ak, av, av, km, vm)


def _swa_sample_kernel(sink_ref, q_ref, kn_ref, vn_ref, kw_ref, vw_ref, km_ref, vm_ref, o_ref, *, bg, ds):
    rows = AT_GROUP * ds
    nk = WINDOW + ds
    ri = lax.broadcasted_iota(jnp.int32, (rows, nk), 0)
    sj = lax.broadcasted_iota(jnp.int32, (rows, nk), 1)
    qpos = PAST_LEN + ri % ds
    kpos = PAST_LEN - WINDOW + sj
    dist = qpos - kpos
    valid = (dist >= 0) & (dist <= WINDOW) & (kpos >= N_META)
    distf = dist.astype(F32)
    gi = lax.broadcasted_iota(jnp.int32, (rows, 1), 0) // ds
    for kv in range(AT_KV_HEADS):
        ks = slice(kv * AT_HEAD_DIM, (kv + 1) * AT_HEAD_DIM)
        slope = jnp.exp2(-(gi + (kv * AT_GROUP + 1)).astype(F32) * (8.0 / AT_HEADS))
        sink = jnp.zeros((rows, 1), F32)
        for g in range(AT_GROUP):
            sink = jnp.where(gi == g, sink_ref[kv * AT_GROUP + g], sink)
        for b in range(bg):
            kall = jnp.concatenate([kw_ref[b, :, ks], kn_ref[b, :, ks]], axis=0).astype(BF16)
            vall = jnp.concatenate([vw_ref[b, :, ks], vn_ref[b, :, ks]], axis=0).astype(BF16)
            kmeta = km_ref[b, :, ks].astype(BF16)
            vmeta = vm_ref[b, :, ks].astype(BF16)
            qs = jnp.concatenate(
                [q_ref[b, :, (kv * AT_GROUP + g) * AT_HEAD_DIM:(kv * AT_GROUP + g + 1) * AT_HEAD_DIM]
                 for g in range(AT_GROUP)], axis=0).astype(BF16)
            lw = lax.dot_general(qs, kall, NT_DIMS, preferred_element_type=F32) - slope * distf
            lw = jnp.where(valid, lw, -jnp.inf)
            lm = lax.dot_general(qs, kmeta, NT_DIMS, preferred_element_type=F32)
            m = jnp.maximum(jnp.maximum(jnp.max(lw, axis=-1, keepdims=True),
                                        jnp.max(lm, axis=-1, keepdims=True)), sink)
            ew = jnp.exp(lw - m)
            em = jnp.exp(lm - m)
            den = (jnp.sum(ew, axis=-1, keepdims=True) + jnp.sum(em, axis=-1, keepdims=True)
                   + jnp.exp(sink - m))
            o = (jnp.dot(ew.astype(BF16), vall, preferred_element_type=F32)
                 + jnp.dot(em.astype(BF16), vmeta, preferred_element_type=F32)) / den
            for g in range(AT_GROUP):
                h = kv * AT_GROUP + g
                o_ref[b, :, h * AT_HEAD_DIM:(h + 1) * AT_HEAD_DIM] = o[g * ds:(g + 1) * ds]


def _swa_sample(sinks, aq, ak, av, win_k, win_v, meta_k, meta_v, bg):
    b, ds = aq.shape[:2]
    spec = lambda a: pl.BlockSpec((bg,) + a.shape[1:], lambda i: (i, 0, 0))
    args = (aq, ak, av, win_k, win_v, meta_k, meta_v)
    return pl.pallas_call(
        functools.partial(_swa_sample_kernel, bg=bg, ds=ds),
        grid=(b // bg,),
        in_specs=[pl.BlockSpec(memory_space=pltpu.SMEM)] + [spec(a) for a in args],
        out_specs=spec(aq),
        out_shape=jax.ShapeDtypeStruct(aq.shape, F32),
        compiler_params=_cparams("parallel"),
        name="swa_sample",
    )(sinks, *args)


def _merge_kernel(x_ref, oa_ref, ob_ref, sga_ref, sgb_ref, wpa_ref, wpb_ref, wo_ref, gain_ref, wq_ref, sk_ref,
                  h2_ref, xn_ref, sc_ref):
    pa = jnp.dot(oa_ref[...].astype(BF16), wpa_ref[...], preferred_element_type=F32)
    pb = jnp.dot(ob_ref[...].astype(BF16), wpb_ref[...], preferred_element_type=F32)
    m = sga_ref[...] * pa + sgb_ref[...] * pb
    h2 = x_ref[...] + jnp.dot(m.astype(BF16), wo_ref[...], preferred_element_type=F32)
    h2_ref[...] = h2
    xn = h2 * lax.rsqrt(jnp.mean(h2 * h2, axis=-1, keepdims=True) + EPS) * gain_ref[...]
    xn_ref[...] = xn
    qp = jnp.dot(xn.astype(BF16), wq_ref[...], preferred_element_type=F32).astype(BF16)
    for hc in range(2 * PEER_HEADS):
        sc_ref[hc] = lax.dot_general(sk_ref[hc], qp[:, hc * PEER_HALF:(hc + 1) * PEER_HALF], NT_DIMS,
                                     preferred_element_type=F32)


def _merge(x, oa, ob, sga, sgb, wpa, wpb, wo, gain, wq, sk, tm):
    n = x.shape[0]
    row = lambda w: pl.BlockSpec((tm, w), lambda i: (i, 0))
    full = lambda a: pl.BlockSpec(a.shape, lambda i: (0,) * a.ndim)
    return pl.pallas_call(
        _merge_kernel,
        grid=(n // tm,),
        in_specs=[row(D_MODEL), row(HG_WIDTH), row(AT_WIDTH), row(D_MODEL), row(D_MODEL),
                  full(wpa), full(wpb), full(wo), full(gain), full(wq), full(sk)],
        out_specs=[row(D_MODEL), row(D_MODEL),
                   pl.BlockSpec((2 * PEER_HEADS, PEER_NKEYS, tm), lambda i: (0, 0, i))],
        out_shape=[jax.ShapeDtypeStruct((n, D_MODEL), F32), jax.ShapeDtypeStruct((n, D_MODEL), F32),
                   jax.ShapeDtypeStruct((2 * PEER_HEADS, PEER_NKEYS, n), F32)],
        compiler_params=_cparams("parallel"),
        name="merge",
    )(x, oa, ob, sga, sgb, wpa, wpb, wo, gain, wq, sk)


def _top16(arrays):
    arrays = list(arrays)
    t = arrays[0].shape[1]
    io16 = lax.broadcasted_iota(jnp.int32, (PEER_TOPK, t), 0)
    ios = [lax.broadcasted_iota(jnp.int32, s.shape, 0).astype(F32) for s in arrays]
    vals = [jnp.zeros((PEER_TOPK, t), F32) for _ in arrays]
    idxs = [jnp.zeros((PEER_TOPK, t), F32) for _ in arrays]
    for j in range(PEER_TOPK):
        for k, (s, io) in enumerate(zip(arrays, ios)):
            m = jnp.max(s, axis=0, keepdims=True)
            idx = jnp.min(jnp.where(s == m, io, float(s.shape[0])), axis=0, keepdims=True)
            vals[k] = jnp.where(io16 == j, m, vals[k])
            idxs[k] = jnp.where(io16 == j, idx, idxs[k])
            arrays[k] = jnp.where(io == idx, -jnp.inf, s)
    return vals, [i.astype(jnp.int32) for i in idxs]


_CAND_A0, _CAND_MID, _CAND_ROWS = PEER_TOPK, PEER_TOPK + 7 * SUBLANES, PEER_TOPK + 8 * SUBLANES


def _candidates(v1, v2):
    t = v1.shape[1]
    io8 = lax.broadcasted_iota(jnp.int32, (SUBLANES, t), 0)
    pieces = [v1[0:1] + v2]
    for a in range(1, SUBLANES):
        keep = PEER_TOPK // (a + 1)
        p = v1[a:a + 1] + v2[0:SUBLANES]
        pieces.append(p if keep >= SUBLANES else jnp.where(io8 < keep, p, -jnp.inf))
    pieces.append(v1[SUBLANES:] + v2[0:1])
    return jnp.concatenate(pieces, axis=0)


def _candidate_ab(pos):
    mid = pos - _CAND_A0
    a = jnp.where(pos < _CAND_A0, 0, jnp.where(pos < _CAND_MID, (mid >> 3) + 1, pos - (_CAND_MID - SUBLANES)))
    b = jnp.where(pos < _CAND_A0, pos, jnp.where(pos < _CAND_MID, mid & (SUBLANES - 1), 0))
    return a, b


def _topk_kernel(s_ref, eidx_ref, gate_ref):
    for h0 in range(0, PEER_HEADS, 2):
        heads = (h0, h0 + 1)
        sub = [_top16([s_ref[2 * h], s_ref[2 * h + 1]]) for h in heads]
        ctops, cposs = _top16([_candidates(vals[0], vals[1]) for vals, _ in sub])
        for h, (_, (i1, i2)), ctop, cpos in zip(heads, sub, ctops, cposs):
            a, b = _candidate_ab(cpos)
            e1 = jnp.zeros_like(a)
            e2 = jnp.zeros_like(b)
            for r in range(PEER_TOPK):
                e1 = jnp.where(a == r, i1[r:r + 1], e1)
                e2 = jnp.where(b == r, i2[r:r + 1], e2)
            ex = jnp.exp(ctop - ctop[0:1])
            sl = slice(h * PEER_TOPK, (h + 1) * PEER_TOPK)
            gate_ref[sl, :] = ex / jnp.sum(ex, axis=0, keepdims=True)
            eidx_ref[sl, :] = e1 * PEER_NKEYS + e2


def _topk(scores_t, tk):
    n = scores_t.shape[2]
    return pl.pallas_call(
        _topk_kernel,
        grid=(n // tk,),
        in_specs=[pl.BlockSpec((2 * PEER_HEADS, PEER_NKEYS, tk), lambda i: (0, 0, i))],
        out_specs=[pl.BlockSpec((PEER_PICKS, tk), lambda i: (0, i))] * 2,
        out_shape=[jax.ShapeDtypeStruct((PEER_PICKS, n), jnp.int32),
                   jax.ShapeDtypeStruct((PEER_PICKS, n), F32)],
        compiler_params=_cparams("parallel"),
        name="topk",
    )(scores_t)


ROW_PARTS = SUBLANES // 2


def _table_rows(tab):
    e = tab.shape[0]
    pairs = tab.astype(BF16).reshape(e, ROW_PARTS, 2, LANES).transpose(0, 1, 3, 2)
    return lax.bitcast_convert_type(pairs, jnp.int32).reshape(e * ROW_PARTS, LANES)


def _load_row(tab_ref, offset):
    words = tab_ref[pl.ds(pl.multiple_of(offset, ROW_PARTS), ROW_PARTS), :]
    return pltpu.bitcast(words, BF16).astype(F32)


_BITREV = tuple(int(format(i, "03b")[::-1], 2) for i in range(SUBLANES))


def _fold_sublanes(a, b, half):
    s = lax.broadcasted_iota(jnp.int32, (SUBLANES, LANES), 0)
    low = (s & half) == 0
    if 2 * half < SUBLANES:
        b = pltpu.roll(b, half, axis=0)
    return jnp.where(low, a, b) + pltpu.roll(jnp.where(low, b, a), SUBLANES - half, axis=0)


def _split_indices(eidx_t):
    n = eidx_t.shape[1]
    return eidx_t.reshape(PEER_OCTETS, SUBLANES, n).transpose(1, 2, 0).reshape(SUBLANES, n * PEER_OCTETS)


def _for_each_index_slot(idx_hbm, sm_refs, sem, tb, process):
    step = pl.program_id(0)
    words = tb * PEER_OCTETS

    def copies(block, slot):
        return [pltpu.make_async_copy(idx_hbm.at[i, pl.ds(block * words, words)], sm_refs[slot][i], sem.at[slot, i])
                for i in range(SUBLANES)]

    def start(block, slot):
        for c in copies(block, slot):
            c.start()

    @pl.when(step == 0)
    def _():
        start(0, 0)

    for slot in range(PEER_SLOTS):
        block = step * PEER_SLOTS + slot
        for c in copies(block, slot):
            c.wait()
        if slot + 1 < PEER_SLOTS:
            start(block + 1, slot + 1)
        else:
            @pl.when(step + 1 < pl.num_programs(0))
            def _():
                start(block + 1, 0)
        process(slot)


def _index_scratch(tb):
    return ([pltpu.SMEM((tb * PEER_OCTETS,), jnp.int32) for _ in range(PEER_SLOTS * SUBLANES)]
            + [pltpu.SemaphoreType.DMA((PEER_SLOTS, SUBLANES))])


def _index_refs(refs):
    return [refs[s * SUBLANES:(s + 1) * SUBLANES] for s in range(PEER_SLOTS)]


def _peer_a_kernel(idx_hbm, x_ref, gate_ref, tab_ref, w_ref, *scratch, tb):
    sm_refs, sem, q_ref, h_ref = _index_refs(scratch), scratch[-3], scratch[-2], scratch[-1]
    lane = lax.broadcasted_iota(jnp.int32, (SUBLANES, tb), 1)

    def place(sums, t):
        h = h_ref[...]
        for octet, col in enumerate(sums):
            rows = slice(octet * SUBLANES, (octet + 1) * SUBLANES)
            h_ref[rows, :] = jnp.where(lane == t, col, h[rows, :])

    def process(slot):
        t0 = slot * tb
        sm = sm_refs[slot]
        h_ref[...] = jnp.zeros_like(h_ref)
        q_ref[...] = jnp.zeros_like(q_ref)

        def lane_sums():
            q = q_ref[...]
            return [jnp.sum(q[o * SUBLANES:(o + 1) * SUBLANES, :], axis=1, keepdims=True) for o in range(PEER_OCTETS)]

        def tok(t, carry):
            sums = lane_sums()
            xv = x_ref[t0 + t]
            for octet in range(PEER_OCTETS):
                base = t * PEER_OCTETS + octet
                tiles = [_load_row(tab_ref, sm[_BITREV[i]][base]) * xv for i in range(SUBLANES)]
                half = SUBLANES // 2
                while len(tiles) > 1:
                    tiles = [_fold_sublanes(tiles[i], tiles[i + 1], half) for i in range(0, len(tiles), 2)]
                    half //= 2
                q_ref[octet * SUBLANES:(octet + 1) * SUBLANES, :] = tiles[0]
            place(sums, t - 1)
            return carry

        lax.fori_loop(0, tb, tok, 0)
        place(lane_sums(), tb - 1)
        h = h_ref[...]
        w_ref[:, t0:t0 + tb] = gate_ref[:, t0:t0 + tb] * (0.5 * h * (1.0 + lax.erf(h * math.sqrt(0.5))))

    _for_each_index_slot(idx_hbm, sm_refs, sem, tb, process)


def _peer_a(eidx_split, xn3, gate_t, tab, tb):
    n = xn3.shape[0]
    span = PEER_SLOTS * tb
    return pl.pallas_call(
        functools.partial(_peer_a_kernel, tb=tb),
        grid=(n // span,),
        in_specs=[pl.BlockSpec(memory_space=pl.ANY),
                  pl.BlockSpec((span, SUBLANES, LANES), lambda i: (i, 0, 0)),
                  pl.BlockSpec((PEER_PICKS, span), lambda i: (0, i)),
                  pl.BlockSpec(tab.shape, lambda i: (0, 0), pipeline_mode=pl.Buffered(1))],
        out_specs=pl.BlockSpec((PEER_PICKS, span), lambda i: (0, i)),
        out_shape=jax.ShapeDtypeStruct((PEER_PICKS, n), F32),
        scratch_shapes=_index_scratch(tb) + [pltpu.VMEM((PEER_PICKS, LANES), F32), pltpu.VMEM((PEER_PICKS, tb), F32)],
        compiler_params=_cparams("arbitrary"),
        name="peer_a",
    )(eidx_split, xn3, gate_t, tab)


def _peer_b_kernel(idx_hbm, wt_ref, h2_ref, gain_ref, tab_ref, y_ref, *scratch, tb):
    sm_refs, sem, wb_refs = _index_refs(scratch), scratch[-3], scratch[-2:]
    nacc = 2

    def process(slot):
        t0 = slot * tb
        sm = sm_refs[slot]
        wt = wt_ref[:, t0:t0 + tb]

        def spread(wb_ref, t):
            wb_ref[...] = jnp.take_along_axis(wt, jnp.full((PEER_PICKS, LANES), t, jnp.int32), axis=1)

        spread(wb_refs[0], 0)

        def pair(tp, carry):
            for par in range(2):
                t = 2 * tp + par
                wb_ref = wb_refs[par]
                spread(wb_refs[1 - par], jnp.minimum(t + 1, tb - 1))
                acc = [jnp.zeros((SUBLANES, LANES), F32) for _ in range(nacc)]
                for octet in range(PEER_OCTETS):
                    base = t * PEER_OCTETS + octet
                    wg = wb_ref[octet * SUBLANES:(octet + 1) * SUBLANES, :]
                    for i in range(SUBLANES):
                        acc[i % nacc] = acc[i % nacc] + wg[i:i + 1, :] * _load_row(tab_ref, sm[i][base])
                total = acc[0]
                for a in acc[1:]:
                    total = total + a
                y_ref[t0 + t] = h2_ref[t0 + t] + total
            return carry

        lax.fori_loop(0, tb // 2, pair, 0)

    _for_each_index_slot(idx_hbm, sm_refs, sem, tb, process)

    y = y_ref[...]
    ms = jnp.sum(jnp.sum(y * y, axis=2, keepdims=True), axis=1, keepdims=True) * (1.0 / D_MODEL)
    y_ref[...] = y * lax.rsqrt(ms + EPS) * gain_ref[...][None]


def _peer_b(eidx_split, w_t, h23, gain3, tab, tb):
    n = h23.shape[0]
    span = PEER_SLOTS * tb
    tok3 = pl.BlockSpec((span, SUBLANES, LANES), lambda i: (i, 0, 0))
    return pl.pallas_call(
        functools.partial(_peer_b_kernel, tb=tb),
        grid=(n // span,),
        in_specs=[pl.BlockSpec(memory_space=pl.ANY),
                  pl.BlockSpec((PEER_PICKS, span), lambda i: (0, i)),
                  tok3, pl.BlockSpec(gain3.shape, lambda i: (0, 0)),
                  pl.BlockSpec(tab.shape, lambda i: (0, 0), pipeline_mode=pl.Buffered(1))],
        out_specs=tok3,
        out_shape=jax.ShapeDtypeStruct(h23.shape, F32),
        scratch_shapes=_index_scratch(tb) + [pltpu.VMEM((PEER_PICKS, LANES), F32) for _ in range(2)],
        compiler_params=_cparams("arbitrary"),
        name="peer_b",
    )(eidx_split, w_t, h23, gain3, tab)


def _mix_and_ffn(x, oa, ob, sga, sgb, wts):
    n = x.shape[0]
    tm, tk, tb = _tile(n, 256), _tile(n, LANES), LANES
    assert n % (PEER_SLOTS * tb) == 0, "PEER kernels take whole groups of token blocks"
    h2, xn, scores_t = _merge(x, oa, ob, sga, sgb, wts["wpa"], wts["wpb"], wts["wo"], wts["gain_ffn"],
                              wts["wq"], wts["sk"], tm)
    eidx_t, gate_t = _topk(scores_t, tk)
    eidx_split = _split_indices(eidx_t * ROW_PARTS)
    w_t = _peer_a(eidx_split, xn.reshape(n, SUBLANES, LANES), gate_t, wts["u"], tb)
    y3 = _peer_b(eidx_split, w_t, h2.reshape(n, SUBLANES, LANES), wts["gain_final"], wts["v"], tb)
    return y3.reshape(n, D_MODEL)


def kernel(x_prompt, x_sample, cache_win_k, cache_win_v, cache_meta_k, cache_meta_v, state_hgrn, meta_tokens,
           norm_mix, w_in, hg_lb, hg_norm, attn_sinks, w_pa, w_pb, w_o, norm_ffn, peer_wq, peer_subkeys,
           peer_u, peer_v, norm_final):
    depth = w_in.shape[0]
    assert depth == 1, "single-layer step only"
    b, seq, _ = x_prompt.shape
    db, ds, _ = x_sample.shape
    l = 0
    lb = jnp.cumsum(jax.nn.softmax(hg_lb.astype(F32), axis=0), axis=0)[l][None, :]
    gain_mix = norm_mix[l][None, :]
    w_in_b = w_in[l].astype(BF16)
    hg_gain = hg_norm[l][None, :]
    sinks = attn_sinks[l].astype(F32)
    wts = dict(
        wpa=w_pa[l].astype(BF16), wpb=w_pb[l].astype(BF16), wo=w_o[l].astype(BF16),
        gain_ffn=norm_ffn[l][None, :], wq=peer_wq[l].astype(BF16),
        sk=peer_subkeys[l].reshape(2 * PEER_HEADS, PEER_NKEYS, PEER_HALF).astype(BF16),
        u=_table_rows(peer_u[l]), v=_table_rows(peer_v[l]),
        gain_final=norm_final.reshape(SUBLANES, LANES),
    )

    mq, mk, mlf, mv, msg, _, km, vm, _, _ = _proj(meta_tokens, gain_mix, lb, w_in_b, N_META)
    zero_state = jnp.zeros((1, HG_HEADS, HG_DK, HG_DK), F32)
    _, s_meta = _hgrn(mq, mk, mlf, mv, msg, hg_gain, zero_state, 1, N_META, N_META)

    xp = x_prompt.reshape(b * seq, D_MODEL)
    q, k, lf, v, sg, aq, ak, av, sga, sgb = _proj(xp, gain_mix, lb, w_in_b, _tile(b * seq, 256))
    oa, s_fin = _hgrn(q, k, lf, v, sg, hg_gain, s_meta, b, HG_CHUNK, _tile(seq, 512))
    ob = _swa_prompt(sinks, aq, ak, av, km, vm, b)
    y_prompt = _mix_and_ffn(xp, oa, ob, sga, sgb, wts).reshape(b, seq, D_MODEL)
    akr = ak.reshape(b, seq, AT_KV_HEADS, AT_HEAD_DIM)
    avr = av.reshape(b, seq, AT_KV_HEADS, AT_HEAD_DIM)
    kmr = jnp.broadcast_to(km.reshape(1, N_META, AT_KV_HEADS, AT_HEAD_DIM), (b, N_META, AT_KV_HEADS, AT_HEAD_DIM))
    vmr = jnp.broadcast_to(vm.reshape(1, N_META, AT_KV_HEADS, AT_HEAD_DIM), (b, N_META, AT_KV_HEADS, AT_HEAD_DIM))

    xs = x_sample.reshape(db * ds, D_MODEL)
    q, k, lf, v, sg, aq, aks, avs, sga, sgb = _proj(xs, gain_mix, lb, w_in_b, _tile(db * ds, 256))
    cs = SUBLANES
    pad = lambda a: jnp.pad(a.reshape(db, ds, HG_WIDTH), ((0, 0), (0, cs - ds), (0, 0))).reshape(db * cs, HG_WIDTH)
    oa, s_new = _hgrn(pad(q), pad(k), pad(lf), pad(v), pad(sg), hg_gain, state_hgrn[l].astype(F32), db, cs, cs)
    oa = oa.reshape(db, cs, HG_WIDTH)[:, :ds].reshape(db * ds, HG_WIDTH)
    ob = _swa_sample(sinks, aq.reshape(db, ds, AT_WIDTH), aks.reshape(db, ds, AT_KV_WIDTH),
                     avs.reshape(db, ds, AT_KV_WIDTH),
                     cache_win_k[l].reshape(db, WINDOW, AT_KV_WIDTH), cache_win_v[l].reshape(db, WINDOW, AT_KV_WIDTH),
                     cache_meta_k[l].reshape(db, N_META, AT_KV_WIDTH), cache_meta_v[l].reshape(db, N_META, AT_KV_WIDTH),
                     8).reshape(db * ds, AT_WIDTH)
    y_sample = _mix_and_ffn(xs, oa, ob, sga, sgb, wts).reshape(db, ds, D_MODEL)

    return (y_prompt, y_sample,
            akr[:, -WINDOW:][None], avr[:, -WINDOW:][None], kmr[None], vmr[None], s_fin[None],
            aks.reshape(1, db, ds, AT_KV_HEADS, AT_HEAD_DIM), avs.reshape(1, db, ds, AT_KV_HEADS, AT_HEAD_DIM),
            s_new[None])
```

```python
import functools
import math

import numpy as np
import jax
import jax.numpy as jnp
from jax import lax
from jax.experimental import pallas as pl
from jax.experimental.pallas import tpu as pltpu

F32 = jnp.float32
BF16 = jnp.bfloat16
EPS = 1e-6
D_MODEL = 1024
N_META = 16
HG_HEADS = 4
HG_DK = 128
HG_WIDTH = HG_HEADS * HG_DK
HG_CHUNK = 64
AT_HEADS = 8
AT_KV_HEADS = 2
AT_GROUP = AT_HEADS // AT_KV_HEADS
AT_HEAD_DIM = 64
AT_WIDTH = AT_HEADS * AT_HEAD_DIM
AT_KV_WIDTH = AT_KV_HEADS * AT_HEAD_DIM
AT_SCALE = AT_HEAD_DIM ** -0.5
WINDOW = 128
BLOCK = 128
PAST_LEN = 16384
PEER_HEADS = 8
PEER_NKEYS = 128
PEER_HALF = 128
PEER_TOPK = 16
PEER_PICKS = PEER_HEADS * PEER_TOPK
PEER_OCTETS = PEER_PICKS // 8
SUBLANES = 8
LANES = 128
PEER_SLOTS = 2
VMEM_LIMIT = 56 * 1024 * 1024

NT_DIMS = (((1,), (1,)), ((), ()))
TN_DIMS = (((0,), (0,)), ((), ()))


def _cparams(*sem):
    return pltpu.CompilerParams(dimension_semantics=sem, vmem_limit_bytes=VMEM_LIMIT)


def _tile(n, pref):
    return pref if n % pref == 0 else n


def _sigmoid(x):
    return 1.0 / (1.0 + jnp.exp(-x))


def _proj_kernel(x_ref, gain_ref, lb_ref, w_ref, q_ref, k_ref, lf_ref, v_ref, sg_ref,
                 aq_ref, ak_ref, av_ref, sga_ref, sgb_ref):
    x = x_ref[...]
    xn = x * lax.rsqrt(jnp.mean(x * x, axis=-1, keepdims=True) + EPS) * gain_ref[...]
    xb = xn.astype(BF16)

    def proj(a, b):
        return jnp.dot(xb, w_ref[:, a:b], preferred_element_type=F32)

    o = 0
    hq = proj(o, o + HG_WIDTH); o += HG_WIDTH
    q_ref[...] = hq * _sigmoid(hq)
    lb = lb_ref[...]
    f = lb + (1.0 - lb) * _sigmoid(proj(o, o + HG_WIDTH)); o += HG_WIDTH
    k_ref[...] = 1.0 - f
    lf_ref[...] = jnp.log(f)
    v_ref[...] = proj(o, o + HG_WIDTH); o += HG_WIDTH
    hg = proj(o, o + HG_WIDTH); o += HG_WIDTH
    sg_ref[...] = hg * _sigmoid(hg)
    aq_ref[...] = proj(o, o + AT_WIDTH) * AT_SCALE; o += AT_WIDTH
    ak_ref[...] = proj(o, o + AT_KV_WIDTH); o += AT_KV_WIDTH
    av_ref[...] = proj(o, o + AT_KV_WIDTH); o += AT_KV_WIDTH
    sga_ref[...] = _sigmoid(proj(o, o + D_MODEL)); o += D_MODEL
    sgb_ref[...] = _sigmoid(proj(o, o + D_MODEL))


def _proj(x, gain, lb, w_bf16, tm):
    n = x.shape[0]
    widths = (HG_WIDTH,) * 5 + (AT_WIDTH, AT_KV_WIDTH, AT_KV_WIDTH, D_MODEL, D_MODEL)
    row = lambda w: pl.BlockSpec((tm, w), lambda i: (i, 0))
    full = lambda a: pl.BlockSpec(a.shape, lambda i: (0,) * a.ndim)
    return pl.pallas_call(
        _proj_kernel,
        grid=(n // tm,),
        in_specs=[row(D_MODEL), full(gain), full(lb), full(w_bf16)],
        out_specs=[row(w) for w in widths],
        out_shape=[jax.ShapeDtypeStruct((n, w), F32) for w in widths],
        compiler_params=_cparams("parallel"),
        name="proj",
    )(x, gain, lb, w_bf16)


def _hgrn_consts(c):
    levels = int(round(math.log2(c)))
    t = np.arange(c)
    le = (t[None, :] <= t[:, None]).astype(np.float32)
    mats = [le]
    masks = []
    for l in range(levels):
        bs, half = 2 << l, 1 << l
        base = (t // bs) * bs
        bnd = base + half - 1
        mats.append(le - (t[None, :] <= bnd[:, None]).astype(np.float32))
        right = (t % bs) >= half
        masks.append((base[:, None] == base[None, :]) & right[:, None] & (~right)[None, :])
    masks.append(np.eye(c, dtype=bool))
    a = np.concatenate(mats, 0)
    a2 = np.concatenate([a, a], 1)
    return jnp.asarray(a2, BF16), jnp.asarray(np.stack(masks).astype(np.float32))


def _hgrn_kernel(a_ref, m_ref, q_ref, k_ref, lf_ref, v_ref, sg_ref, gain_ref, s0_ref, o_ref, sfin_ref, *st_ref,
                 c, nchunk):
    levels = int(round(math.log2(c)))
    i = pl.program_id(1)

    @pl.when(i == 0)
    def _():
        for h in range(HG_HEADS):
            st_ref[h][...] = s0_ref[0, h].T

    gain = gain_ref[...]

    def chunk(ci, carry):
        r0 = pl.multiple_of(ci * c, c)
        rows = pl.ds(r0, c)
        heads = range(HG_HEADS)
        cols = [slice(h * HG_DK, (h + 1) * HG_DK) for h in heads]
        lf = lf_ref[rows, :]
        hi = lf.astype(BF16)
        lo = (lf - hi.astype(F32)).astype(BF16)
        r_all = jnp.dot(a_ref[...], jnp.concatenate([hi, lo], axis=0), preferred_element_type=F32)
        q = [q_ref[rows, cs] for cs in cols]
        k = [k_ref[rows, cs] for cs in cols]
        vb = [v_ref[rows, cs].astype(BF16) for cs in cols]
        st = [st_ref[h][...] for h in heads]
        r = [r_all[:, cs] for cs in cols]
        cum = [r[h][0:c] for h in heads]
        o = [lax.dot_general((q[h] * jnp.exp(cum[h])).astype(BF16), st[h].astype(BF16), NT_DIMS,
                             preferred_element_type=F32) for h in heads]
        att = [m_ref[levels] * lax.dot_general(q[h].astype(BF16), k[h].astype(BF16), NT_DIMS,
                                               preferred_element_type=F32) for h in heads]
        for l in range(levels):
            for h in heads:
                e = r[h][(l + 1) * c:(l + 2) * c]
                aq = (q[h] * jnp.exp(jnp.minimum(e, 0.0))).astype(BF16)
                ak = (k[h] * jnp.exp(jnp.minimum(-e, 0.0))).astype(BF16)
                att[h] = att[h] + m_ref[l] * lax.dot_general(aq, ak, NT_DIMS, preferred_element_type=F32)
        o = [o[h] + jnp.dot(att[h].astype(BF16), vb[h], preferred_element_type=F32) for h in heads]
        for h in heads:
            last = cum[h][c - 1:c]
            kdec = (k[h] * jnp.exp(last - cum[h])).astype(BF16)
            st_ref[h][...] = st[h] * jnp.exp(last) + lax.dot_general(vb[h], kdec, TN_DIMS,
                                                                    preferred_element_type=F32)
        for h in heads:
            on = o[h] * lax.rsqrt(jnp.mean(o[h] * o[h], axis=-1, keepdims=True) + EPS) * gain
            o_ref[rows, cols[h]] = on * sg_ref[rows, cols[h]]
        return carry

    lax.fori_loop(0, nchunk, chunk, 0)

    @pl.when(i == pl.num_programs(1) - 1)
    def _():
        for h in range(HG_HEADS):
            sfin_ref[0, h] = st_ref[h][...].T


def _hgrn(q, k, lf, v, sg, gain, s0, batch, c, ct):
    n = q.shape[0]
    t = n // batch
    steps = t // ct
    a2, masks = _hgrn_consts(c)
    row = pl.BlockSpec((ct, HG_WIDTH), lambda b, i: (b * steps + i, 0))
    full = lambda a: pl.BlockSpec(a.shape, lambda b, i: (0,) * a.ndim)
    shared = s0.shape[0] == 1
    sspec = pl.BlockSpec((1, HG_HEADS, HG_DK, HG_DK), (lambda b, i: (0, 0, 0, 0)) if shared else (lambda b, i: (b, 0, 0, 0)))
    return pl.pallas_call(
        functools.partial(_hgrn_kernel, c=c, nchunk=ct // c),
        grid=(batch, steps),
        in_specs=[full(a2), full(masks), row, row, row, row, row, full(gain), sspec],
        out_specs=[row, pl.BlockSpec((1, HG_HEADS, HG_DK, HG_DK), lambda b, i: (b, 0, 0, 0))],
        out_shape=[jax.ShapeDtypeStruct((n, HG_WIDTH), F32),
                   jax.ShapeDtypeStruct((batch, HG_HEADS, HG_DK, HG_DK), F32)],
        scratch_shapes=[pltpu.VMEM((HG_DK, HG_DK), F32) for _ in range(HG_HEADS)],
        compiler_params=_cparams("parallel", "arbitrary"),
        name="hgrn",
    )(a2, masks, q, k, lf, v, sg, gain, s0)


def _swa_prompt_kernel(sink_ref, q_ref, kc_ref, kp_ref, vc_ref, vp_ref, km_ref, vm_ref, o_ref):
    n = pl.program_id(1)
    rows = AT_GROUP * BLOCK
    ri = lax.broadcasted_iota(jnp.int32, (rows, 2 * BLOCK), 0)
    sj = lax.broadcasted_iota(jnp.int32, (rows, 2 * BLOCK), 1)
    dist = ri % BLOCK + BLOCK - sj
    valid = (dist >= 0) & (dist <= WINDOW) & ((sj >= BLOCK) | (n > 0))
    distf = dist.astype(F32)
    gi = lax.broadcasted_iota(jnp.int32, (rows, 1), 0) // BLOCK
    for kv in range(AT_KV_HEADS):
        ks = slice(kv * AT_HEAD_DIM, (kv + 1) * AT_HEAD_DIM)
        kband = jnp.concatenate([kp_ref[:, ks], kc_ref[:, ks]], axis=0).astype(BF16)
        vband = jnp.concatenate([vp_ref[:, ks], vc_ref[:, ks]], axis=0).astype(BF16)
        kmeta = km_ref[:, ks].astype(BF16)
        vmeta = vm_ref[:, ks].astype(BF16)
        heads = [kv * AT_GROUP + g for g in range(AT_GROUP)]
        slope = jnp.exp2(-(gi + (kv * AT_GROUP + 1)).astype(F32) * (8.0 / AT_HEADS))
        sink = jnp.zeros((rows, 1), F32)
        for g, h in enumerate(heads):
            sink = jnp.where(gi == g, sink_ref[h], sink)
        qs = jnp.concatenate([q_ref[:, h * AT_HEAD_DIM:(h + 1) * AT_HEAD_DIM] for h in heads],
                             axis=0).astype(BF16)
        lband = lax.dot_general(qs, kband, NT_DIMS, preferred_element_type=F32) - slope * distf
        lband = jnp.where(valid, lband, -jnp.inf)
        lmeta = lax.dot_general(qs, kmeta, NT_DIMS, preferred_element_type=F32)
        m = jnp.maximum(jnp.maximum(jnp.max(lband, axis=-1, keepdims=True),
                                    jnp.max(lmeta, axis=-1, keepdims=True)), sink)
        eb = jnp.exp(lband - m)
        em = jnp.exp(lmeta - m)
        den = (jnp.sum(eb, axis=-1, keepdims=True) + jnp.sum(em, axis=-1, keepdims=True)
               + jnp.exp(sink - m))
        o = (jnp.dot(eb.astype(BF16), vband, preferred_element_type=F32)
             + jnp.dot(em.astype(BF16), vmeta, preferred_element_type=F32)) / den
        for g, h in enumerate(heads):
            o_ref[:, h * AT_HEAD_DIM:(h + 1) * AT_HEAD_DIM] = o[g * BLOCK:(g + 1) * BLOCK]


def _swa_prompt(sinks, aq, ak, av, km, vm, batch):
    n = aq.shape[0]
    nb = n // batch // BLOCK
    cur = lambda w: pl.BlockSpec((BLOCK, w), lambda b, i: (b * nb + i, 0))
    prev = lambda w: pl.BlockSpec((BLOCK, w), lambda b, i: (b * nb + jnp.maximum(i - 1, 0), 0))
    meta = pl.BlockSpec((N_META, AT_KV_WIDTH), lambda b, i: (0, 0))
    return pl.pallas_call(
        _swa_prompt_kernel,
        grid=(batch, nb),
        in_specs=[pl.BlockSpec(memory_space=pltpu.SMEM), cur(AT_WIDTH), cur(AT_KV_WIDTH), prev(AT_KV_WIDTH),
                  cur(AT_KV_WIDTH), prev(AT_KV_WIDTH), meta, meta],
        out_specs=cur(AT_WIDTH),
        out_shape=jax.ShapeDtypeStruct((n, AT_WIDTH), F32),
        compiler_params=_cparams("parallel", "parallel"),
        name="swa_prompt",
    )(sinks, aq, ak, ak, av, av, km, vm)


def _swa_sample_kernel(sink_ref, q_ref, kn_ref, vn_ref, kw_ref, vw_ref, km_ref, vm_ref, o_ref, *, bg, ds):
    rows = AT_GROUP * ds
    nk = WINDOW + ds
    ri = lax.broadcasted_iota(jnp.int32, (rows, nk), 0)
    sj = lax.broadcasted_iota(jnp.int32, (rows, nk), 1)
    qpos = PAST_LEN + ri % ds
    kpos = PAST_LEN - WINDOW + sj
    dist = qpos - kpos
    valid = (dist >= 0) & (dist <= WINDOW) & (kpos >= N_META)
    distf = dist.astype(F32)
    gi = lax.broadcasted_iota(jnp.int32, (rows, 1), 0) // ds
    for kv in range(AT_KV_HEADS):
        ks = slice(kv * AT_HEAD_DIM, (kv + 1) * AT_HEAD_DIM)
        slope = jnp.exp2(-(gi + (kv * AT_GROUP + 1)).astype(F32) * (8.0 / AT_HEADS))
        sink = jnp.zeros((rows, 1), F32)
        for g in range(AT_GROUP):
            sink = jnp.where(gi == g, sink_ref[kv * AT_GROUP + g], sink)
        for b in range(bg):
            kall = jnp.concatenate([kw_ref[b, :, ks], kn_ref[b, :, ks]], axis=0).astype(BF16)
            vall = jnp.concatenate([vw_ref[b, :, ks], vn_ref[b, :, ks]], axis=0).astype(BF16)
            kmeta = km_ref[b, :, ks].astype(BF16)
            vmeta = vm_ref[b, :, ks].astype(BF16)
            qs = jnp.concatenate(
                [q_ref[b, :, (kv * AT_GROUP + g) * AT_HEAD_DIM:(kv * AT_GROUP + g + 1) * AT_HEAD_DIM]
                 for g in range(AT_GROUP)], axis=0).astype(BF16)
            lw = lax.dot_general(qs, kall, NT_DIMS, preferred_element_type=F32) - slope * distf
            lw = jnp.where(valid, lw, -jnp.inf)
            lm = lax.dot_general(qs, kmeta, NT_DIMS, preferred_element_type=F32)
            m = jnp.maximum(jnp.maximum(jnp.max(lw, axis=-1, keepdims=True),
                                        jnp.max(lm, axis=-1, keepdims=True)), sink)
            ew = jnp.exp(lw - m)
            em = jnp.exp(lm - m)
            den = (jnp.sum(ew, axis=-1, keepdims=True) + jnp.sum(em, axis=-1, keepdims=True)
                   + jnp.exp(sink - m))
            o = (jnp.dot(ew.astype(BF16), vall, preferred_element_type=F32)
                 + jnp.dot(em.astype(BF16), vmeta, preferred_element_type=F32)) / den
            for g in range(AT_GROUP):
                h = kv * AT_GROUP + g
                o_ref[b, :, h * AT_HEAD_DIM:(h + 1) * AT_HEAD_DIM] = o[g * ds:(g + 1) * ds]


def _swa_sample(sinks, aq, ak, av, win_k, win_v, meta_k, meta_v, bg):
    b, ds = aq.shape[:2]
    spec = lambda a: pl.BlockSpec((bg,) + a.shape[1:], lambda i: (i, 0, 0))
    args = (aq, ak, av, win_k, win_v, meta_k, meta_v)
    return pl.pallas_call(
        functools.partial(_swa_sample_kernel, bg=bg, ds=ds),
        grid=(b // bg,),
        in_specs=[pl.BlockSpec(memory_space=pltpu.SMEM)] + [spec(a) for a in args],
        out_specs=spec(aq),
        out_shape=jax.ShapeDtypeStruct(aq.shape, F32),
        compiler_params=_cparams("parallel"),
        name="swa_sample",
    )(sinks, *args)


def _merge_kernel(x_ref, oa_ref, ob_ref, sga_ref, sgb_ref, wpa_ref, wpb_ref, wo_ref, gain_ref, wq_ref, sk_ref,
                  h2_ref, xn_ref, sc_ref):
    pa = jnp.dot(oa_ref[...].astype(BF16), wpa_ref[...], preferred_element_type=F32)
    pb = jnp.dot(ob_ref[...].astype(BF16), wpb_ref[...], preferred_element_type=F32)
    m = sga_ref[...] * pa + sgb_ref[...] * pb
    h2 = x_ref[...] + jnp.dot(m.astype(BF16), wo_ref[...], preferred_element_type=F32)
    h2_ref[...] = h2
    xn = h2 * lax.rsqrt(jnp.mean(h2 * h2, axis=-1, keepdims=True) + EPS) * gain_ref[...]
    xn_ref[...] = xn
    qp = jnp.dot(xn.astype(BF16), wq_ref[...], preferred_element_type=F32).astype(BF16)
    for hc in range(2 * PEER_HEADS):
        sc_ref[hc] = lax.dot_general(sk_ref[hc], qp[:, hc * PEER_HALF:(hc + 1) * PEER_HALF], NT_DIMS,
                                     preferred_element_type=F32)


def _merge(x, oa, ob, sga, sgb, wpa, wpb, wo, gain, wq, sk, tm):
    n = x.shape[0]
    row = lambda w: pl.BlockSpec((tm, w), lambda i: (i, 0))
    full = lambda a: pl.BlockSpec(a.shape, lambda i: (0,) * a.ndim)
    return pl.pallas_call(
        _merge_kernel,
        grid=(n // tm,),
        in_specs=[row(D_MODEL), row(HG_WIDTH), row(AT_WIDTH), row(D_MODEL), row(D_MODEL),
                  full(wpa), full(wpb), full(wo), full(gain), full(wq), full(sk)],
        out_specs=[row(D_MODEL), row(D_MODEL),
                   pl.BlockSpec((2 * PEER_HEADS, PEER_NKEYS, tm), lambda i: (0, 0, i))],
        out_shape=[jax.ShapeDtypeStruct((n, D_MODEL), F32), jax.ShapeDtypeStruct((n, D_MODEL), F32),
                   jax.ShapeDtypeStruct((2 * PEER_HEADS, PEER_NKEYS, n), F32)],
        compiler_params=_cparams("parallel"),
        name="merge",
    )(x, oa, ob, sga, sgb, wpa, wpb, wo, gain, wq, sk)


def _top16(arrays):
    arrays = list(arrays)
    t = arrays[0].shape[1]
    io16 = lax.broadcasted_iota(jnp.int32, (PEER_TOPK, t), 0)
    ios = [lax.broadcasted_iota(jnp.int32, s.shape, 0).astype(F32) for s in arrays]
    vals = [jnp.zeros((PEER_TOPK, t), F32) for _ in arrays]
    idxs = [jnp.zeros((PEER_TOPK, t), F32) for _ in arrays]
    for j in range(PEER_TOPK):
        for k, (s, io) in enumerate(zip(arrays, ios)):
            m = jnp.max(s, axis=0, keepdims=True)
            idx = jnp.min(jnp.where(s == m, io, float(s.shape[0])), axis=0, keepdims=True)
            vals[k] = jnp.where(io16 == j, m, vals[k])
            idxs[k] = jnp.where(io16 == j, idx, idxs[k])
            arrays[k] = jnp.where(io == idx, -jnp.inf, s)
    return vals, [i.astype(jnp.int32) for i in idxs]


_CAND_A0, _CAND_MID, _CAND_ROWS = PEER_TOPK, PEER_TOPK + 7 * SUBLANES, PEER_TOPK + 8 * SUBLANES


def _candidates(v1, v2):
    t = v1.shape[1]
    io8 = lax.broadcasted_iota(jnp.int32, (SUBLANES, t), 0)
    pieces = [v1[0:1] + v2]
    for a in range(1, SUBLANES):
        keep = PEER_TOPK // (a + 1)
        p = v1[a:a + 1] + v2[0:SUBLANES]
        pieces.append(p if keep >= SUBLANES else jnp.where(io8 < keep, p, -jnp.inf))
    pieces.append(v1[SUBLANES:] + v2[0:1])
    return jnp.concatenate(pieces, axis=0)


def _candidate_ab(pos):
    mid = pos - _CAND_A0
    a = jnp.where(pos < _CAND_A0, 0, jnp.where(pos < _CAND_MID, (mid >> 3) + 1, pos - (_CAND_MID - SUBLANES)))
    b = jnp.where(pos < _CAND_A0, pos, jnp.where(pos < _CAND_MID, mid & (SUBLANES - 1), 0))
    return a, b


def _topk_kernel(s_ref, eidx_ref, gate_ref):
    for h0 in range(0, PEER_HEADS, 2):
        heads = (h0, h0 + 1)
        sub = [_top16([s_ref[2 * h], s_ref[2 * h + 1]]) for h in heads]
        ctops, cposs = _top16([_candidates(vals[0], vals[1]) for vals, _ in sub])
        for h, (_, (i1, i2)), ctop, cpos in zip(heads, sub, ctops, cposs):
            a, b = _candidate_ab(cpos)
            e1 = jnp.zeros_like(a)
            e2 = jnp.zeros_like(b)
            for r in range(PEER_TOPK):
                e1 = jnp.where(a == r, i1[r:r + 1], e1)
                e2 = jnp.where(b == r, i2[r:r + 1], e2)
            ex = jnp.exp(ctop - ctop[0:1])
            sl = slice(h * PEER_TOPK, (h + 1) * PEER_TOPK)
            gate_ref[sl, :] = ex / jnp.sum(ex, axis=0, keepdims=True)
            eidx_ref[sl, :] = e1 * PEER_NKEYS + e2


def _topk(scores_t, tk):
    n = scores_t.shape[2]
    return pl.pallas_call(
        _topk_kernel,
        grid=(n // tk,),
        in_specs=[pl.BlockSpec((2 * PEER_HEADS, PEER_NKEYS, tk), lambda i: (0, 0, i))],
        out_specs=[pl.BlockSpec((PEER_PICKS, tk), lambda i: (0, i))] * 2,
        out_shape=[jax.ShapeDtypeStruct((PEER_PICKS, n), jnp.int32),
                   jax.ShapeDtypeStruct((PEER_PICKS, n), F32)],
        compiler_params=_cparams("parallel"),
        name="topk",
    )(scores_t)


ROW_PARTS = SUBLANES // 2


def _pack_rows_kernel(x_ref, o_ref):
    x = x_ref[...].astype(BF16)
    for s in range(ROW_PARTS):
        even = x[:, (2 * s) * LANES:(2 * s + 1) * LANES]
        odd = x[:, (2 * s + 1) * LANES:(2 * s + 2) * LANES]
        both = jnp.stack([even, odd], axis=1).reshape(2 * x.shape[0], LANES)
        o_ref[:, s * LANES:(s + 1) * LANES] = pltpu.bitcast(both, jnp.int32)


def _table_rows(tab):
    e = tab.shape[0]
    tm = _tile(e, 256)
    words = pl.pallas_call(
        _pack_rows_kernel,
        grid=(e // tm,),
        in_specs=[pl.BlockSpec((tm, D_MODEL), lambda i: (i, 0))],
        out_specs=pl.BlockSpec((tm, ROW_PARTS * LANES), lambda i: (i, 0)),
        out_shape=jax.ShapeDtypeStruct((e, ROW_PARTS * LANES), jnp.int32),
        compiler_params=_cparams("parallel"),
        name="pack_rows",
    )(tab)
    return words.reshape(e * ROW_PARTS, LANES)


def _load_row(tab_ref, offset):
    words = tab_ref[pl.ds(pl.multiple_of(offset, ROW_PARTS), ROW_PARTS), :]
    return pltpu.bitcast(words, BF16).astype(F32)


_BITREV = tuple(int(format(i, "03b")[::-1], 2) for i in range(SUBLANES))


def _fold_sublanes(a, b, half):
    s = lax.broadcasted_iota(jnp.int32, (SUBLANES, LANES), 0)
    low = (s & half) == 0
    if 2 * half < SUBLANES:
        b = pltpu.roll(b, half, axis=0)
    return jnp.where(low, a, b) + pltpu.roll(jnp.where(low, b, a), SUBLANES - half, axis=0)


def _split_indices(eidx_t):
    n = eidx_t.shape[1]
    return eidx_t.reshape(PEER_OCTETS, SUBLANES, n).transpose(1, 2, 0).reshape(SUBLANES, n * PEER_OCTETS)


def _for_each_index_slot(idx_hbm, sm_refs, sem, tb, process):
    step = pl.program_id(0)
    words = tb * PEER_OCTETS

    def copies(block, slot):
        return [pltpu.make_async_copy(idx_hbm.at[i, pl.ds(block * words, words)], sm_refs[slot][i], sem.at[slot, i])
                for i in range(SUBLANES)]

    def start(block, slot):
        for c in copies(block, slot):
            c.start()

    @pl.when(step == 0)
    def _():
        start(0, 0)

    for slot in range(PEER_SLOTS):
        block = step * PEER_SLOTS + slot
        for c in copies(block, slot):
            c.wait()
        if slot + 1 < PEER_SLOTS:
            start(block + 1, slot + 1)
        else:
            @pl.when(step + 1 < pl.num_programs(0))
            def _():
                start(block + 1, 0)
        process(slot)


def _index_scratch(tb):
    return ([pltpu.SMEM((tb * PEER_OCTETS,), jnp.int32) for _ in range(PEER_SLOTS * SUBLANES)]
            + [pltpu.SemaphoreType.DMA((PEER_SLOTS, SUBLANES))])


def _index_refs(refs):
    return [refs[s * SUBLANES:(s + 1) * SUBLANES] for s in range(PEER_SLOTS)]


def _peer_a_kernel(idx_hbm, x_ref, gate_ref, tab_ref, w_ref, *scratch, tb):
    sm_refs, sem, q_ref, h_ref = _index_refs(scratch), scratch[-3], scratch[-2], scratch[-1]
    lane = lax.broadcasted_iota(jnp.int32, (SUBLANES, tb), 1)

    def place(sums, t):
        h = h_ref[...]
        for octet, col in enumerate(sums):
            rows = slice(octet * SUBLANES, (octet + 1) * SUBLANES)
            h_ref[rows, :] = jnp.where(lane == t, col, h[rows, :])

    def process(slot):
        t0 = slot * tb
        sm = sm_refs[slot]
        h_ref[...] = jnp.zeros_like(h_ref)
        q_ref[...] = jnp.zeros_like(q_ref)

        def lane_sums():
            q = q_ref[...]
            return [jnp.sum(q[o * SUBLANES:(o + 1) * SUBLANES, :], axis=1, keepdims=True) for o in range(PEER_OCTETS)]

        def tok(t, carry):
            sums = lane_sums()
            xv = x_ref[t0 + t]
            for octet in range(PEER_OCTETS):
                base = t * PEER_OCTETS + octet
                tiles = [_load_row(tab_ref, sm[_BITREV[i]][base]) * xv for i in range(SUBLANES)]
                half = SUBLANES // 2
                while len(tiles) > 1:
                    tiles = [_fold_sublanes(tiles[i], tiles[i + 1], half) for i in range(0, len(tiles), 2)]
                    half //= 2
                q_ref[octet * SUBLANES:(octet + 1) * SUBLANES, :] = tiles[0]
            place(sums, t - 1)
            return carry

        lax.fori_loop(0, tb, tok, 0)
        place(lane_sums(), tb - 1)
        h = h_ref[...]
        w_ref[:, t0:t0 + tb] = gate_ref[:, t0:t0 + tb] * (0.5 * h * (1.0 + lax.erf(h * math.sqrt(0.5))))

    _for_each_index_slot(idx_hbm, sm_refs, sem, tb, process)


def _peer_a(eidx_split, xn3, gate_t, tab, tb):
    n = xn3.shape[0]
    span = PEER_SLOTS * tb
    return pl.pallas_call(
        functools.partial(_peer_a_kernel, tb=tb),
        grid=(n // span,),
        in_specs=[pl.BlockSpec(memory_space=pl.ANY),
                  pl.BlockSpec((span, SUBLANES, LANES), lambda i: (i, 0, 0)),
                  pl.BlockSpec((PEER_PICKS, span), lambda i: (0, i)),
                  pl.BlockSpec(tab.shape, lambda i: (0, 0), pipeline_mode=pl.Buffered(1))],
        out_specs=pl.BlockSpec((PEER_PICKS, span), lambda i: (0, i)),
        out_shape=jax.ShapeDtypeStruct((PEER_PICKS, n), F32),
        scratch_shapes=_index_scratch(tb) + [pltpu.VMEM((PEER_PICKS, LANES), F32), pltpu.VMEM((PEER_PICKS, tb), F32)],
        compiler_params=_cparams("arbitrary"),
        name="peer_a",
    )(eidx_split, xn3, gate_t, tab)


def _peer_b_kernel(idx_hbm, wt_ref, h2_ref, gain_ref, tab_ref, y_ref, *scratch, tb):
    sm_refs, sem, wb_refs = _index_refs(scratch), scratch[-3], scratch[-2:]
    nacc = 2

    def process(slot):
        t0 = slot * tb
        sm = sm_refs[slot]
        wt = wt_ref[:, t0:t0 + tb]

        def spread(wb_ref, t):
            wb_ref[...] = jnp.take_along_axis(wt, jnp.full((PEER_PICKS, LANES), t, jnp.int32), axis=1)

        spread(wb_refs[0], 0)

        def pair(tp, carry):
            for par in range(2):
                t = 2 * tp + par
                wb_ref = wb_refs[par]
                spread(wb_refs[1 - par], jnp.minimum(t + 1, tb - 1))
                acc = [jnp.zeros((SUBLANES, LANES), F32) for _ in range(nacc)]
                for octet in range(PEER_OCTETS):
                    base = t * PEER_OCTETS + octet
                    wg = wb_ref[octet * SUBLANES:(octet + 1) * SUBLANES, :]
                    for i in range(SUBLANES):
                        acc[i % nacc] = acc[i % nacc] + wg[i:i + 1, :] * _load_row(tab_ref, sm[i][base])
                total = acc[0]
                for a in acc[1:]:
                    total = total + a
                y_ref[t0 + t] = h2_ref[t0 + t] + total
            return carry

        lax.fori_loop(0, tb // 2, pair, 0)

    _for_each_index_slot(idx_hbm, sm_refs, sem, tb, process)

    y = y_ref[...]
    ms = jnp.sum(jnp.sum(y * y, axis=2, keepdims=True), axis=1, keepdims=True) * (1.0 / D_MODEL)
    y_ref[...] = y * lax.rsqrt(ms + EPS) * gain_ref[...][None]


def _peer_b(eidx_split, w_t, h23, gain3, tab, tb):
    n = h23.shape[0]
    span = PEER_SLOTS * tb
    tok3 = pl.BlockSpec((span, SUBLANES, LANES), lambda i: (i, 0, 0))
    return pl.pallas_call(
        functools.partial(_peer_b_kernel, tb=tb),
        grid=(n // span,),
        in_specs=[pl.BlockSpec(memory_space=pl.ANY),
                  pl.BlockSpec((PEER_PICKS, span), lambda i: (0, i)),
                  tok3, pl.BlockSpec(gain3.shape, lambda i: (0, 0)),
                  pl.BlockSpec(tab.shape, lambda i: (0, 0), pipeline_mode=pl.Buffered(1))],
        out_specs=tok3,
        out_shape=jax.ShapeDtypeStruct(h23.shape, F32),
        scratch_shapes=_index_scratch(tb) + [pltpu.VMEM((PEER_PICKS, LANES), F32) for _ in range(2)],
        compiler_params=_cparams("arbitrary"),
        name="peer_b",
    )(eidx_split, w_t, h23, gain3, tab)


def _mix_and_ffn(x, oa, ob, sga, sgb, wts):
    n = x.shape[0]
    tm, tk, tb = _tile(n, 256), _tile(n, LANES), LANES
    assert n % (PEER_SLOTS * tb) == 0, "PEER kernels take whole groups of token blocks"
    h2, xn, scores_t = _merge(x, oa, ob, sga, sgb, wts["wpa"], wts["wpb"], wts["wo"], wts["gain_ffn"],
                              wts["wq"], wts["sk"], tm)
    eidx_t, gate_t = _topk(scores_t, tk)
    eidx_split = _split_indices(eidx_t * ROW_PARTS)
    w_t = _peer_a(eidx_split, xn.reshape(n, SUBLANES, LANES), gate_t, wts["u"], tb)
    y3 = _peer_b(eidx_split, w_t, h2.reshape(n, SUBLANES, LANES), wts["gain_final"], wts["v"], tb)
    return y3.reshape(n, D_MODEL)


def kernel(x_prompt, x_sample, cache_win_k, cache_win_v, cache_meta_k, cache_meta_v, state_hgrn, meta_tokens,
           norm_mix, w_in, hg_lb, hg_norm, attn_sinks, w_pa, w_pb, w_o, norm_ffn, peer_wq, peer_subkeys,
           peer_u, peer_v, norm_final):
    depth = w_in.shape[0]
    assert depth == 1, "single-layer step only"
    b, seq, _ = x_prompt.shape
    db, ds, _ = x_sample.shape
    l = 0
    lb = jnp.cumsum(jax.nn.softmax(hg_lb.astype(F32), axis=0), axis=0)[l][None, :]
    gain_mix = norm_mix[l][None, :]
    w_in_b = w_in[l].astype(BF16)
    hg_gain = hg_norm[l][None, :]
    sinks = attn_sinks[l].astype(F32)
    wts = dict(
        wpa=w_pa[l].astype(BF16), wpb=w_pb[l].astype(BF16), wo=w_o[l].astype(BF16),
        gain_ffn=norm_ffn[l][None, :], wq=peer_wq[l].astype(BF16),
        sk=peer_subkeys[l].reshape(2 * PEER_HEADS, PEER_NKEYS, PEER_HALF).astype(BF16),
        u=_table_rows(peer_u[l]), v=_table_rows(peer_v[l]),
        gain_final=norm_final.reshape(SUBLANES, LANES),
    )

    mq, mk, mlf, mv, msg, _, km, vm, _, _ = _proj(meta_tokens, gain_mix, lb, w_in_b, N_META)
    zero_state = jnp.zeros((1, HG_HEADS, HG_DK, HG_DK), F32)
    _, s_meta = _hgrn(mq, mk, mlf, mv, msg, hg_gain, zero_state, 1, N_META, N_META)

    xp = x_prompt.reshape(b * seq, D_MODEL)
    q, k, lf, v, sg, aq, ak, av, sga, sgb = _proj(xp, gain_mix, lb, w_in_b, _tile(b * seq, 256))
    oa, s_fin = _hgrn(q, k, lf, v, sg, hg_gain, s_meta, b, HG_CHUNK, _tile(seq, 512))
    ob = _swa_prompt(sinks, aq, ak, av, km, vm, b)
    y_prompt = _mix_and_ffn(xp, oa, ob, sga, sgb, wts).reshape(b, seq, D_MODEL)
    akr = ak.reshape(b, seq, AT_KV_HEADS, AT_HEAD_DIM)
    avr = av.reshape(b, seq, AT_KV_HEADS, AT_HEAD_DIM)
    kmr = jnp.broadcast_to(km.reshape(1, N_META, AT_KV_HEADS, AT_HEAD_DIM), (b, N_META, AT_KV_HEADS, AT_HEAD_DIM))
    vmr = jnp.broadcast_to(vm.reshape(1, N_META, AT_KV_HEADS, AT_HEAD_DIM), (b, N_META, AT_KV_HEADS, AT_HEAD_DIM))

    xs = x_sample.reshape(db * ds, D_MODEL)
    q, k, lf, v, sg, aq, aks, avs, sga, sgb = _proj(xs, gain_mix, lb, w_in_b, _tile(db * ds, 256))
    cs = SUBLANES
    pad = lambda a: jnp.pad(a.reshape(db, ds, HG_WIDTH), ((0, 0), (0, cs - ds), (0, 0))).reshape(db * cs, HG_WIDTH)
    oa, s_new = _hgrn(pad(q), pad(k), pad(lf), pad(v), pad(sg), hg_gain, state_hgrn[l].astype(F32), db, cs, cs)
    oa = oa.reshape(db, cs, HG_WIDTH)[:, :ds].reshape(db * ds, HG_WIDTH)
    ob = _swa_sample(sinks, aq.reshape(db, ds, AT_WIDTH), aks.reshape(db, ds, AT_KV_WIDTH),
                     avs.reshape(db, ds, AT_KV_WIDTH),
                     cache_win_k[l].reshape(db, WINDOW, AT_KV_WIDTH), cache_win_v[l].reshape(db, WINDOW, AT_KV_WIDTH),
                     cache_meta_k[l].reshape(db, N_META, AT_KV_WIDTH), cache_meta_v[l].reshape(db, N_META, AT_KV_WIDTH),
                     8).reshape(db * ds, AT_WIDTH)
    y_sample = _mix_and_ffn(xs, oa, ob, sga, sgb, wts).reshape(db, ds, D_MODEL)

    return (y_prompt, y_sample,
            akr[:, -WINDOW:][None], avr[:, -WINDOW:][None], kmr[None], vmr[None], s_fin[None],
            aks.reshape(1, db, ds, AT_KV_HEADS, AT_HEAD_DIM), avs.reshape(1, db, ds, AT_KV_HEADS, AT_HEAD_DIM),
            s_new[None])
```

```python
import functools
import math

import numpy as np
import jax
import jax.numpy as jnp
from jax import lax
from jax.experimental import pallas as pl
from jax.experimental.pallas import tpu as pltpu

F32 = jnp.float32
BF16 = jnp.bfloat16
EPS = 1e-6
D_MODEL = 1024
N_META = 16
HG_HEADS = 4
HG_DK = 128
HG_WIDTH = HG_HEADS * HG_DK
HG_CHUNK = 64
HG_CHUNK_UNROLL = 8
AT_HEADS = 8
AT_KV_HEADS = 2
AT_GROUP = AT_HEADS // AT_KV_HEADS
AT_HEAD_DIM = 64
AT_WIDTH = AT_HEADS * AT_HEAD_DIM
AT_KV_WIDTH = AT_KV_HEADS * AT_HEAD_DIM
AT_SCALE = AT_HEAD_DIM ** -0.5
WINDOW = 128
BLOCK = 128
PAST_LEN = 16384
PEER_HEADS = 8
PEER_NKEYS = 128
PEER_HALF = 128
PEER_TOPK = 16
PEER_PICKS = PEER_HEADS * PEER_TOPK
SUBLANES = 8
LANES = 128
PEER_OCTETS = PEER_PICKS // SUBLANES
PEER_SLOTS = 2
VMEM_LIMIT = 56 * 1024 * 1024

NT_DIMS = (((1,), (1,)), ((), ()))
TN_DIMS = (((0,), (0,)), ((), ()))


def _cparams(*sem):
    return pltpu.CompilerParams(dimension_semantics=sem, vmem_limit_bytes=VMEM_LIMIT)


def _tile(n, pref):
    return pref if n % pref == 0 else n


def _sigmoid(x):
    return 1.0 / (1.0 + jnp.exp(-x))


def _proj_kernel(x_ref, gain_ref, lb_ref, w_ref, q_ref, k_ref, lf_ref, v_ref, sg_ref,
                 aq_ref, ak_ref, av_ref, sga_ref, sgb_ref):
    x = x_ref[...]
    xn = x * lax.rsqrt(jnp.mean(x * x, axis=-1, keepdims=True) + EPS) * gain_ref[...]
    xb = xn.astype(BF16)

    def proj(a, b):
        return jnp.dot(xb, w_ref[:, a:b], preferred_element_type=F32)

    o = 0
    hq = proj(o, o + HG_WIDTH); o += HG_WIDTH
    q_ref[...] = hq * _sigmoid(hq)
    lb = lb_ref[...]
    f = lb + (1.0 - lb) * _sigmoid(proj(o, o + HG_WIDTH)); o += HG_WIDTH
    k_ref[...] = 1.0 - f
    lf_ref[...] = jnp.log(f)
    v_ref[...] = proj(o, o + HG_WIDTH); o += HG_WIDTH
    hg = proj(o, o + HG_WIDTH); o += HG_WIDTH
    sg_ref[...] = hg * _sigmoid(hg)
    aq_ref[...] = proj(o, o + AT_WIDTH) * AT_SCALE; o += AT_WIDTH
    ak_ref[...] = proj(o, o + AT_KV_WIDTH); o += AT_KV_WIDTH
    av_ref[...] = proj(o, o + AT_KV_WIDTH); o += AT_KV_WIDTH
    sga_ref[...] = _sigmoid(proj(o, o + D_MODEL)); o += D_MODEL
    sgb_ref[...] = _sigmoid(proj(o, o + D_MODEL))


def _proj(x, gain, lb, w_bf16, tm):
    n = x.shape[0]
    widths = (HG_WIDTH,) * 5 + (AT_WIDTH, AT_KV_WIDTH, AT_KV_WIDTH, D_MODEL, D_MODEL)
    row = lambda w: pl.BlockSpec((tm, w), lambda i: (i, 0))
    full = lambda a: pl.BlockSpec(a.shape, lambda i: (0,) * a.ndim)
    return pl.pallas_call(
        _proj_kernel,
        grid=(n // tm,),
        in_specs=[row(D_MODEL), full(gain), full(lb), full(w_bf16)],
        out_specs=[row(w) for w in widths],
        out_shape=[jax.ShapeDtypeStruct((n, w), F32) for w in widths],
        compiler_params=_cparams("parallel"),
        name="proj",
    )(x, gain, lb, w_bf16)


def _hgrn_consts(c):
    levels = int(round(math.log2(c)))
    t = np.arange(c)
    le = (t[None, :] <= t[:, None]).astype(np.float32)
    mats = [le]
    masks = []
    for l in range(levels):
        bs, half = 2 << l, 1 << l
        base = (t // bs) * bs
        bnd = base + half - 1
        mats.append(le - (t[None, :] <= bnd[:, None]).astype(np.float32))
        right = (t % bs) >= half
        masks.append((base[:, None] == base[None, :]) & right[:, None] & (~right)[None, :])
    masks.append(np.eye(c, dtype=bool))
    a = np.concatenate(mats, 0)
    a2 = np.concatenate([a, a], 1)
    return jnp.asarray(a2, BF16), jnp.asarray(np.stack(masks).astype(np.float32))


def _hgrn_kernel(a_ref, m_ref, q_ref, k_ref, lf_ref, v_ref, sg_ref, gain_ref, s0_ref, o_ref, sfin_ref, *st_ref,
                 c, nchunk):
    levels = int(round(math.log2(c)))
    i = pl.program_id(1)

    @pl.when(i == 0)
    def _():
        for h in range(HG_HEADS):
            st_ref[h][...] = s0_ref[0, h].T

    gain = gain_ref[...]

    def chunk(ci, carry):
        r0 = pl.multiple_of(ci * c, c)
        rows = pl.ds(r0, c)
        heads = range(HG_HEADS)
        cols = [slice(h * HG_DK, (h + 1) * HG_DK) for h in heads]
        lf = lf_ref[rows, :]
        hi = lf.astype(BF16)
        lo = (lf - hi.astype(F32)).astype(BF16)
        r_all = jnp.dot(a_ref[...], jnp.concatenate([hi, lo], axis=0), preferred_element_type=F32)
        q = [q_ref[rows, cs] for cs in cols]
        k = [k_ref[rows, cs] for cs in cols]
        vb = [v_ref[rows, cs].astype(BF16) for cs in cols]
        st = [st_ref[h][...] for h in heads]
        r = [r_all[:, cs] for cs in cols]
        cum = [r[h][0:c] for h in heads]
        o = [lax.dot_general((q[h] * jnp.exp(cum[h])).astype(BF16), st[h].astype(BF16), NT_DIMS,
                             preferred_element_type=F32) for h in heads]
        att = [m_ref[levels] * lax.dot_general(q[h].astype(BF16), k[h].astype(BF16), NT_DIMS,
                                               preferred_element_type=F32) for h in heads]
        for l in range(levels):
            for h in heads:
                e = r[h][(l + 1) * c:(l + 2) * c]
                aq = (q[h] * jnp.exp(jnp.minimum(e, 0.0))).astype(BF16)
                ak = (k[h] * jnp.exp(jnp.minimum(-e, 0.0))).astype(BF16)
                att[h] = att[h] + m_ref[l] * lax.dot_general(aq, ak, NT_DIMS, preferred_element_type=F32)
        o = [o[h] + jnp.dot(att[h].astype(BF16), vb[h], preferred_element_type=F32) for h in heads]
        for h in heads:
            last = cum[h][c - 1:c]
            kdec = (k[h] * jnp.exp(last - cum[h])).astype(BF16)
            st_ref[h][...] = st[h] * jnp.exp(last) + lax.dot_general(vb[h], kdec, TN_DIMS,
                                                                    preferred_element_type=F32)
        for h in heads:
            on = o[h] * lax.rsqrt(jnp.mean(o[h] * o[h], axis=-1, keepdims=True) + EPS) * gain
            o_ref[rows, cols[h]] = on * sg_ref[rows, cols[h]]
        return carry

    unroll = math.gcd(nchunk, HG_CHUNK_UNROLL)

    def chunks(cj, carry):
        for u in range(unroll):
            chunk(cj * unroll + u, carry)
        return carry

    lax.fori_loop(0, nchunk // unroll, chunks, 0)

    @pl.when(i == pl.num_programs(1) - 1)
    def _():
        for h in range(HG_HEADS):
            sfin_ref[0, h] = st_ref[h][...].T


def _hgrn(q, k, lf, v, sg, gain, s0, batch, c, ct):
    n = q.shape[0]
    t = n // batch
    steps = t // ct
    a2, masks = _hgrn_consts(c)
    row = pl.BlockSpec((ct, HG_WIDTH), lambda b, i: (b * steps + i, 0))
    full = lambda a: pl.BlockSpec(a.shape, lambda b, i: (0,) * a.ndim)
    shared = s0.shape[0] == 1
    sspec = pl.BlockSpec((1, HG_HEADS, HG_DK, HG_DK), (lambda b, i: (0, 0, 0, 0)) if shared else (lambda b, i: (b, 0, 0, 0)))
    return pl.pallas_call(
        functools.partial(_hgrn_kernel, c=c, nchunk=ct // c),
        grid=(batch, steps),
        in_specs=[full(a2), full(masks), row, row, row, row, row, full(gain), sspec],
        out_specs=[row, pl.BlockSpec((1, HG_HEADS, HG_DK, HG_DK), lambda b, i: (b, 0, 0, 0))],
        out_shape=[jax.ShapeDtypeStruct((n, HG_WIDTH), F32),
                   jax.ShapeDtypeStruct((batch, HG_HEADS, HG_DK, HG_DK), F32)],
        scratch_shapes=[pltpu.VMEM((HG_DK, HG_DK), F32) for _ in range(HG_HEADS)],
        compiler_params=_cparams("parallel", "arbitrary"),
        name="hgrn",
    )(a2, masks, q, k, lf, v, sg, gain, s0)


def _swa_prompt_kernel(sink_ref, q_ref, kc_ref, kp_ref, vc_ref, vp_ref, km_ref, vm_ref, o_ref):
    n = pl.program_id(1)
    rows = AT_GROUP * BLOCK
    ri = lax.broadcasted_iota(jnp.int32, (rows, 2 * BLOCK), 0)
    sj = lax.broadcasted_iota(jnp.int32, (rows, 2 * BLOCK), 1)
    dist = ri % BLOCK + BLOCK - sj
    valid = (dist >= 0) & (dist <= WINDOW) & ((sj >= BLOCK) | (n > 0))
    distf = dist.astype(F32)
    gi = lax.broadcasted_iota(jnp.int32, (rows, 1), 0) // BLOCK
    for kv in range(AT_KV_HEADS):
        ks = slice(kv * AT_HEAD_DIM, (kv + 1) * AT_HEAD_DIM)
        kband = jnp.concatenate([kp_ref[:, ks], kc_ref[:, ks]], axis=0).astype(BF16)
        vband = jnp.concatenate([vp_ref[:, ks], vc_ref[:, ks]], axis=0).astype(BF16)
        kmeta = km_ref[:, ks].astype(BF16)
        vmeta = vm_ref[:, ks].astype(BF16)
        heads = [kv * AT_GROUP + g for g in range(AT_GROUP)]
        slope = jnp.exp2(-(gi + (kv * AT_GROUP + 1)).astype(F32) * (8.0 / AT_HEADS))
        sink = jnp.zeros((rows, 1), F32)
        for g, h in enumerate(heads):
            sink = jnp.where(gi == g, sink_ref[h], sink)
        qs = jnp.concatenate([q_ref[:, h * AT_HEAD_DIM:(h + 1) * AT_HEAD_DIM] for h in heads],
                             axis=0).astype(BF16)
        lband = lax.dot_general(qs, kband, NT_DIMS, preferred_element_type=F32) - slope * distf
        lband = jnp.where(valid, lband, -jnp.inf)
        lmeta = lax.dot_general(qs, kmeta, NT_DIMS, preferred_element_type=F32)
        m = jnp.maximum(jnp.maximum(jnp.max(lband, axis=-1, keepdims=True),
                                    jnp.max(lmeta, axis=-1, keepdims=True)), sink)
        eb = jnp.exp(lband - m)
        em = jnp.exp(lmeta - m)
        den = (jnp.sum(eb, axis=-1, keepdims=True) + jnp.sum(em, axis=-1, keepdims=True)
               + jnp.exp(sink - m))
        o = (jnp.dot(eb.astype(BF16), vband, preferred_element_type=F32)
             + jnp.dot(em.astype(BF16), vmeta, preferred_element_type=F32)) / den
        for g, h in enumerate(heads):
            o_ref[:, h * AT_HEAD_DIM:(h + 1) * AT_HEAD_DIM] = o[g * BLOCK:(g + 1) * BLOCK]


def _swa_prompt(sinks, aq, ak, av, km, vm, batch):
    n = aq.shape[0]
    nb = n // batch // BLOCK
    cur = lambda w: pl.BlockSpec((BLOCK, w), lambda b, i: (b * nb + i, 0))
    prev = lambda w: pl.BlockSpec((BLOCK, w), lambda b, i: (b * nb + jnp.maximum(i - 1, 0), 0))
    meta = pl.BlockSpec((N_META, AT_KV_WIDTH), lambda b, i: (0, 0))
    return pl.pallas_call(
        _swa_prompt_kernel,
        grid=(batch, nb),
        in_specs=[pl.BlockSpec(memory_space=pltpu.SMEM), cur(AT_WIDTH), cur(AT_KV_WIDTH), prev(AT_KV_WIDTH),
                  cur(AT_KV_WIDTH), prev(AT_KV_WIDTH), meta, meta],
        out_specs=cur(AT_WIDTH),
        out_shape=jax.ShapeDtypeStruct((n, AT_WIDTH), F32),
        compiler_params=_cparams("parallel", "parallel"),
        name="swa_prompt",
    )(sinks, aq, ak, ak, av, av, km, vm)


def _swa_sample_kernel(sink_ref, q_ref, kn_ref, vn_ref, kw_ref, vw_ref, km_ref, vm_ref, o_ref, *, bg, ds):
    rows = AT_GROUP * ds
    nk = WINDOW + ds
    ri = lax.broadcasted_iota(jnp.int32, (rows, nk), 0)
    sj = lax.broadcasted_iota(jnp.int32, (rows, nk), 1)
    qpos = PAST_LEN + ri % ds
    kpos = PAST_LEN - WINDOW + sj
    dist = qpos - kpos
    valid = (dist >= 0) & (dist <= WINDOW) & (kpos >= N_META)
    distf = dist.astype(F32)
    gi = lax.broadcasted_iota(jnp.int32, (rows, 1), 0) // ds
    for kv in range(AT_KV_HEADS):
        ks = slice(kv * AT_HEAD_DIM, (kv + 1) * AT_HEAD_DIM)
        slope = jnp.exp2(-(gi + (kv * AT_GROUP + 1)).astype(F32) * (8.0 / AT_HEADS))
        sink = jnp.zeros((rows, 1), F32)
        for g in range(AT_GROUP):
            sink = jnp.where(gi == g, sink_ref[kv * AT_GROUP + g], sink)
        for b in range(bg):
            kall = jnp.concatenate([kw_ref[b, :, ks], kn_ref[b, :, ks]], axis=0).astype(BF16)
            vall = jnp.concatenate([vw_ref[b, :, ks], vn_ref[b, :, ks]], axis=0).astype(BF16)
            kmeta = km_ref[b, :, ks].astype(BF16)
            vmeta = vm_ref[b, :, ks].astype(BF16)
            qs = jnp.concatenate(
                [q_ref[b, :, (kv * AT_GROUP + g) * AT_HEAD_DIM:(kv * AT_GROUP + g + 1) * AT_HEAD_DIM]
                 for g in range(AT_GROUP)], axis=0).astype(BF16)
            lw = lax.dot_general(qs, kall, NT_DIMS, preferred_element_type=F32) - slope * distf
            lw = jnp.where(valid, lw, -jnp.inf)
            lm = lax.dot_general(qs, kmeta, NT_DIMS, preferred_element_type=F32)
            m = jnp.maximum(jnp.maximum(jnp.max(lw, axis=-1, keepdims=True),
                                        jnp.max(lm, axis=-1, keepdims=True)), sink)
            ew = jnp.exp(lw - m)
            em = jnp.exp(lm - m)
            den = (jnp.sum(ew, axis=-1, keepdims=True) + jnp.sum(em, axis=-1, keepdims=True)
                   + jnp.exp(sink - m))
            o = (jnp.dot(ew.astype(BF16), vall, preferred_element_type=F32)
                 + jnp.dot(em.astype(BF16), vmeta, preferred_element_type=F32)) / den
            for g in range(AT_GROUP):
                h = kv * AT_GROUP + g
                o_ref[b, :, h * AT_HEAD_DIM:(h + 1) * AT_HEAD_DIM] = o[g * ds:(g + 1) * ds]


def _swa_sample(sinks, aq, ak, av, win_k, win_v, meta_k, meta_v, bg):
    b, ds = aq.shape[:2]
    spec = lambda a: pl.BlockSpec((bg,) + a.shape[1:], lambda i: (i, 0, 0))
    args = (aq, ak, av, win_k, win_v, meta_k, meta_v)
    return pl.pallas_call(
        functools.partial(_swa_sample_kernel, bg=bg, ds=ds),
        grid=(b // bg,),
        in_specs=[pl.BlockSpec(memory_space=pltpu.SMEM)] + [spec(a) for a in args],
        out_specs=spec(aq),
        out_shape=jax.ShapeDtypeStruct(aq.shape, F32),
        compiler_params=_cparams("parallel"),
        name="swa_sample",
    )(sinks, *args)


def _merge_kernel(x_ref, oa_ref, ob_ref, sga_ref, sgb_ref, wpa_ref, wpb_ref, wo_ref, gain_ref, wq_ref, sk_ref,
                  h2_ref, xn_ref, sc_ref):
    pa = jnp.dot(oa_ref[...].astype(BF16), wpa_ref[...], preferred_element_type=F32)
    pb = jnp.dot(ob_ref[...].astype(BF16), wpb_ref[...], preferred_element_type=F32)
    m = sga_ref[...] * pa + sgb_ref[...] * pb
    h2 = x_ref[...] + jnp.dot(m.astype(BF16), wo_ref[...], preferred_element_type=F32)
    h2_ref[...] = h2
    xn = h2 * lax.rsqrt(jnp.mean(h2 * h2, axis=-1, keepdims=True) + EPS) * gain_ref[...]
    xn_ref[...] = xn
    qp = jnp.dot(xn.astype(BF16), wq_ref[...], preferred_element_type=F32).astype(BF16)
    for hc in range(2 * PEER_HEADS):
        sc_ref[hc] = lax.dot_general(sk_ref[hc], qp[:, hc * PEER_HALF:(hc + 1) * PEER_HALF], NT_DIMS,
                                     preferred_element_type=F32)


def _merge(x, oa, ob, sga, sgb, wpa, wpb, wo, gain, wq, sk, tm):
    n = x.shape[0]
    row = lambda w: pl.BlockSpec((tm, w), lambda i: (i, 0))
    full = lambda a: pl.BlockSpec(a.shape, lambda i: (0,) * a.ndim)
    return pl.pallas_call(
        _merge_kernel,
        grid=(n // tm,),
        in_specs=[row(D_MODEL), row(HG_WIDTH), row(AT_WIDTH), row(D_MODEL), row(D_MODEL),
                  full(wpa), full(wpb), full(wo), full(gain), full(wq), full(sk)],
        out_specs=[row(D_MODEL), row(D_MODEL),
                   pl.BlockSpec((2 * PEER_HEADS, PEER_NKEYS, tm), lambda i: (0, 0, i))],
        out_shape=[jax.ShapeDtypeStruct((n, D_MODEL), F32), jax.ShapeDtypeStruct((n, D_MODEL), F32),
                   jax.ShapeDtypeStruct((2 * PEER_HEADS, PEER_NKEYS, n), F32)],
        compiler_params=_cparams("parallel"),
        name="merge",
    )(x, oa, ob, sga, sgb, wpa, wpb, wo, gain, wq, sk)


def _top16(arrays):
    arrays = list(arrays)
    t = arrays[0].shape[1]
    io16 = lax.broadcasted_iota(jnp.int32, (PEER_TOPK, t), 0)
    ios = [lax.broadcasted_iota(jnp.int32, s.shape, 0).astype(F32) for s in arrays]
    vals = [jnp.zeros((PEER_TOPK, t), F32) for _ in arrays]
    idxs = [jnp.zeros((PEER_TOPK, t), F32) for _ in arrays]
    for j in range(PEER_TOPK):
        for k, (s, io) in enumerate(zip(arrays, ios)):
            m = jnp.max(s, axis=0, keepdims=True)
            idx = jnp.min(jnp.where(s == m, io, float(s.shape[0])), axis=0, keepdims=True)
            vals[k] = jnp.where(io16 == j, m, vals[k])
            idxs[k] = jnp.where(io16 == j, idx, idxs[k])
            arrays[k] = jnp.where(io == idx, -jnp.inf, s)
    return vals, [i.astype(jnp.int32) for i in idxs]


_CAND_A0, _CAND_MID, _CAND_ROWS = PEER_TOPK, PEER_TOPK + 7 * SUBLANES, PEER_TOPK + 8 * SUBLANES


def _candidates(v1, v2):
    t = v1.shape[1]
    io8 = lax.broadcasted_iota(jnp.int32, (SUBLANES, t), 0)
    pieces = [v1[0:1] + v2]
    for a in range(1, SUBLANES):
        keep = PEER_TOPK // (a + 1)
        p = v1[a:a + 1] + v2[0:SUBLANES]
        pieces.append(p if keep >= SUBLANES else jnp.where(io8 < keep, p, -jnp.inf))
    pieces.append(v1[SUBLANES:] + v2[0:1])
    return jnp.concatenate(pieces, axis=0)


def _candidate_ab(pos):
    mid = pos - _CAND_A0
    a = jnp.where(pos < _CAND_A0, 0, jnp.where(pos < _CAND_MID, (mid >> 3) + 1, pos - (_CAND_MID - SUBLANES)))
    b = jnp.where(pos < _CAND_A0, pos, jnp.where(pos < _CAND_MID, mid & (SUBLANES - 1), 0))
    return a, b


def _topk_kernel(s_ref, eidx_ref, gate_ref):
    for h0 in range(0, PEER_HEADS, 2):
        heads = (h0, h0 + 1)
        sub = [_top16([s_ref[2 * h], s_ref[2 * h + 1]]) for h in heads]
        ctops, cposs = _top16([_candidates(vals[0], vals[1]) for vals, _ in sub])
        for h, (_, (i1, i2)), ctop, cpos in zip(heads, sub, ctops, cposs):
            a, b = _candidate_ab(cpos)
            e1 = jnp.zeros_like(a)
            e2 = jnp.zeros_like(b)
            for r in range(PEER_TOPK):
                e1 = jnp.where(a == r, i1[r:r + 1], e1)
                e2 = jnp.where(b == r, i2[r:r + 1], e2)
            ex = jnp.exp(ctop - ctop[0:1])
            sl = slice(h * PEER_TOPK, (h + 1) * PEER_TOPK)
            gate_ref[sl, :] = ex / jnp.sum(ex, axis=0, keepdims=True)
            eidx_ref[sl, :] = e1 * PEER_NKEYS + e2


def _topk(scores_t, tk):
    n = scores_t.shape[2]
    return pl.pallas_call(
        _topk_kernel,
        grid=(n // tk,),
        in_specs=[pl.BlockSpec((2 * PEER_HEADS, PEER_NKEYS, tk), lambda i: (0, 0, i))],
        out_specs=[pl.BlockSpec((PEER_PICKS, tk), lambda i: (0, i))] * 2,
        out_shape=[jax.ShapeDtypeStruct((PEER_PICKS, n), jnp.int32),
                   jax.ShapeDtypeStruct((PEER_PICKS, n), F32)],
        compiler_params=_cparams("parallel"),
        name="topk",
    )(scores_t)


ROW_PARTS = SUBLANES // 2


def _pack_rows_kernel(x_ref, o_ref):
    x = x_ref[...].astype(BF16)
    for s in range(ROW_PARTS):
        even = x[:, (2 * s) * LANES:(2 * s + 1) * LANES]
        odd = x[:, (2 * s + 1) * LANES:(2 * s + 2) * LANES]
        both = jnp.stack([even, odd], axis=1).reshape(2 * x.shape[0], LANES)
        o_ref[:, s * LANES:(s + 1) * LANES] = pltpu.bitcast(both, jnp.int32)


def _table_rows(tab):
    e = tab.shape[0]
    tm = _tile(e, 256)
    words = pl.pallas_call(
        _pack_rows_kernel,
        grid=(e // tm,),
        in_specs=[pl.BlockSpec((tm, D_MODEL), lambda i: (i, 0))],
        out_specs=pl.BlockSpec((tm, ROW_PARTS * LANES), lambda i: (i, 0)),
        out_shape=jax.ShapeDtypeStruct((e, ROW_PARTS * LANES), jnp.int32),
        compiler_params=_cparams("parallel"),
        name="pack_rows",
    )(tab)
    return words.reshape(e * ROW_PARTS, LANES)


def _load_row(tab_ref, offset):
    words = tab_ref[pl.ds(pl.multiple_of(offset, ROW_PARTS), ROW_PARTS), :]
    return pltpu.bitcast(words, BF16).astype(F32)


_BITREV = tuple(int(format(i, "03b")[::-1], 2) for i in range(SUBLANES))


def _fold_sublanes(a, b, half):
    s = lax.broadcasted_iota(jnp.int32, (SUBLANES, LANES), 0)
    low = (s & half) == 0
    if 2 * half < SUBLANES:
        b = pltpu.roll(b, half, axis=0)
    return jnp.where(low, a, b) + pltpu.roll(jnp.where(low, b, a), SUBLANES - half, axis=0)


def _split_indices(eidx_t):
    n = eidx_t.shape[1]
    return eidx_t.reshape(PEER_OCTETS, SUBLANES, n).transpose(1, 2, 0).reshape(SUBLANES, n * PEER_OCTETS)


def _for_each_index_slot(idx_hbm, sm_refs, sem, tb, process):
    step = pl.program_id(0)
    words = tb * PEER_OCTETS

    def copies(block, slot):
        return [pltpu.make_async_copy(idx_hbm.at[i, pl.ds(block * words, words)], sm_refs[slot][i], sem.at[slot, i])
                for i in range(SUBLANES)]

    def start(block, slot):
        for c in copies(block, slot):
            c.start()

    @pl.when(step == 0)
    def _():
        start(0, 0)

    for slot in range(PEER_SLOTS):
        block = step * PEER_SLOTS + slot
        for c in copies(block, slot):
            c.wait()
        if slot + 1 < PEER_SLOTS:
            start(block + 1, slot + 1)
        else:
            @pl.when(step + 1 < pl.num_programs(0))
            def _():
                start(block + 1, 0)
        process(slot)


def _index_scratch(tb):
    return ([pltpu.SMEM((tb * PEER_OCTETS,), jnp.int32) for _ in range(PEER_SLOTS * SUBLANES)]
            + [pltpu.SemaphoreType.DMA((PEER_SLOTS, SUBLANES))])


def _index_refs(refs):
    return [refs[s * SUBLANES:(s + 1) * SUBLANES] for s in range(PEER_SLOTS)]


def _peer_a_kernel(idx_hbm, x_ref, gate_ref, tab_ref, w_ref, *scratch, tb):
    sm_refs, sem, q_ref, h_ref = _index_refs(scratch), scratch[-3], scratch[-2], scratch[-1]
    lane = lax.broadcasted_iota(jnp.int32, (SUBLANES, tb), 1)

    def place(sums, t):
        h = h_ref[...]
        for octet, col in enumerate(sums):
            rows = slice(octet * SUBLANES, (octet + 1) * SUBLANES)
            h_ref[rows, :] = jnp.where(lane == t, col, h[rows, :])

    def process(slot):
        t0 = slot * tb
        sm = sm_refs[slot]
        h_ref[...] = jnp.zeros_like(h_ref)
        q_ref[...] = jnp.zeros_like(q_ref)

        def lane_sums():
            q = q_ref[...]
            return [jnp.sum(q[o * SUBLANES:(o + 1) * SUBLANES, :], axis=1, keepdims=True) for o in range(PEER_OCTETS)]

        def tok(t, carry):
            sums = lane_sums()
            xv = x_ref[t0 + t]
            for octet in range(PEER_OCTETS):
                base = t * PEER_OCTETS + octet
                tiles = [_load_row(tab_ref, sm[_BITREV[i]][base]) * xv for i in range(SUBLANES)]
                half = SUBLANES // 2
                while len(tiles) > 1:
                    tiles = [_fold_sublanes(tiles[i], tiles[i + 1], half) for i in range(0, len(tiles), 2)]
                    half //= 2
                q_ref[octet * SUBLANES:(octet + 1) * SUBLANES, :] = tiles[0]
            place(sums, t - 1)
            return carry

        lax.fori_loop(0, tb, tok, 0)
        place(lane_sums(), tb - 1)
        h = h_ref[...]
        w_ref[:, t0:t0 + tb] = gate_ref[:, t0:t0 + tb] * (0.5 * h * (1.0 + lax.erf(h * math.sqrt(0.5))))

    _for_each_index_slot(idx_hbm, sm_refs, sem, tb, process)


def _peer_a(eidx_split, xn3, gate_t, tab, tb):
    n = xn3.shape[0]
    span = PEER_SLOTS * tb
    return pl.pallas_call(
        functools.partial(_peer_a_kernel, tb=tb),
        grid=(n // span,),
        in_specs=[pl.BlockSpec(memory_space=pl.ANY),
                  pl.BlockSpec((span, SUBLANES, LANES), lambda i: (i, 0, 0)),
                  pl.BlockSpec((PEER_PICKS, span), lambda i: (0, i)),
                  pl.BlockSpec(tab.shape, lambda i: (0, 0), pipeline_mode=pl.Buffered(1))],
        out_specs=pl.BlockSpec((PEER_PICKS, span), lambda i: (0, i)),
        out_shape=jax.ShapeDtypeStruct((PEER_PICKS, n), F32),
        scratch_shapes=_index_scratch(tb) + [pltpu.VMEM((PEER_PICKS, LANES), F32), pltpu.VMEM((PEER_PICKS, tb), F32)],
        compiler_params=_cparams("arbitrary"),
        name="peer_a",
    )(eidx_split, xn3, gate_t, tab)


def _peer_b_kernel(idx_hbm, wt_ref, h2_ref, gain_ref, tab_ref, y_ref, *scratch, tb):
    sm_refs, sem, wb_refs = _index_refs(scratch), scratch[-3], scratch[-2:]
    nacc = 2

    def process(slot):
        t0 = slot * tb
        sm = sm_refs[slot]
        wt = wt_ref[:, t0:t0 + tb]

        def spread(wb_ref, t):
            wb_ref[...] = jnp.take_along_axis(wt, jnp.full((PEER_PICKS, LANES), t, jnp.int32), axis=1)

        spread(wb_refs[0], 0)

        def pair(tp, carry):
            for par in range(2):
                t = 2 * tp + par
                wb_ref = wb_refs[par]
                spread(wb_refs[1 - par], jnp.minimum(t + 1, tb - 1))
                acc = [jnp.zeros((SUBLANES, LANES), F32) for _ in range(nacc)]
                for octet in range(PEER_OCTETS):
                    base = t * PEER_OCTETS + octet
                    wg = wb_ref[octet * SUBLANES:(octet + 1) * SUBLANES, :]
                    for i in range(SUBLANES):
                        acc[i % nacc] = acc[i % nacc] + wg[i:i + 1, :] * _load_row(tab_ref, sm[i][base])
                total = acc[0]
                for a in acc[1:]:
                    total = total + a
                y_ref[t0 + t] = h2_ref[t0 + t] + total
            return carry

        lax.fori_loop(0, tb // 2, pair, 0)

    _for_each_index_slot(idx_hbm, sm_refs, sem, tb, process)

    y = y_ref[...]
    ms = jnp.sum(jnp.sum(y * y, axis=2, keepdims=True), axis=1, keepdims=True) * (1.0 / D_MODEL)
    y_ref[...] = y * lax.rsqrt(ms + EPS) * gain_ref[...][None]


def _peer_b(eidx_split, w_t, h23, gain3, tab, tb):
    n = h23.shape[0]
    span = PEER_SLOTS * tb
    tok3 = pl.BlockSpec((span, SUBLANES, LANES), lambda i: (i, 0, 0))
    return pl.pallas_call(
        functools.partial(_peer_b_kernel, tb=tb),
        grid=(n // span,),
        in_specs=[pl.BlockSpec(memory_space=pl.ANY),
                  pl.BlockSpec((PEER_PICKS, span), lambda i: (0, i)),
                  tok3, pl.BlockSpec(gain3.shape, lambda i: (0, 0)),
                  pl.BlockSpec(tab.shape, lambda i: (0, 0), pipeline_mode=pl.Buffered(1))],
        out_specs=tok3,
        out_shape=jax.ShapeDtypeStruct(h23.shape, F32),
        scratch_shapes=_index_scratch(tb) + [pltpu.VMEM((PEER_PICKS, LANES), F32) for _ in range(2)],
        compiler_params=_cparams("arbitrary"),
        name="peer_b",
    )(eidx_split, w_t, h23, gain3, tab)


def _mix_and_ffn(x, oa, ob, sga, sgb, wts):
    n = x.shape[0]
    tm, tk, tb = _tile(n, 256), _tile(n, LANES), LANES
    assert n % (PEER_SLOTS * tb) == 0, "PEER kernels take whole groups of token blocks"
    h2, xn, scores_t = _merge(x, oa, ob, sga, sgb, wts["wpa"], wts["wpb"], wts["wo"], wts["gain_ffn"],
                              wts["wq"], wts["sk"], tm)
    eidx_t, gate_t = _topk(scores_t, tk)
    eidx_split = _split_indices(eidx_t * ROW_PARTS)
    w_t = _peer_a(eidx_split, xn.reshape(n, SUBLANES, LANES), gate_t, wts["u"], tb)
    y3 = _peer_b(eidx_split, w_t, h2.reshape(n, SUBLANES, LANES), wts["gain_final"], wts["v"], tb)
    return y3.reshape(n, D_MODEL)


def kernel(x_prompt, x_sample, cache_win_k, cache_win_v, cache_meta_k, cache_meta_v, state_hgrn, meta_tokens,
           norm_mix, w_in, hg_lb, hg_norm, attn_sinks, w_pa, w_pb, w_o, norm_ffn, peer_wq, peer_subkeys,
           peer_u, peer_v, norm_final):
    depth = w_in.shape[0]
    assert depth == 1, "single-layer step only"
    b, seq, _ = x_prompt.shape
    db, ds, _ = x_sample.shape
    l = 0
    lb = jnp.cumsum(jax.nn.softmax(hg_lb.astype(F32), axis=0), axis=0)[l][None, :]
    gain_mix = norm_mix[l][None, :]
    w_in_b = w_in[l].astype(BF16)
    hg_gain = hg_norm[l][None, :]
    sinks = attn_sinks[l].astype(F32)
    wts = dict(
        wpa=w_pa[l].astype(BF16), wpb=w_pb[l].astype(BF16), wo=w_o[l].astype(BF16),
        gain_ffn=norm_ffn[l][None, :], wq=peer_wq[l].astype(BF16),
        sk=peer_subkeys[l].reshape(2 * PEER_HEADS, PEER_NKEYS, PEER_HALF).astype(BF16),
        u=_table_rows(peer_u[l]), v=_table_rows(peer_v[l]),
        gain_final=norm_final.reshape(SUBLANES, LANES),
    )

    mq, mk, mlf, mv, msg, _, km, vm, _, _ = _proj(meta_tokens, gain_mix, lb, w_in_b, N_META)
    zero_state = jnp.zeros((1, HG_HEADS, HG_DK, HG_DK), F32)
    _, s_meta = _hgrn(mq, mk, mlf, mv, msg, hg_gain, zero_state, 1, N_META, N_META)

    xp = x_prompt.reshape(b * seq, D_MODEL)
    q, k, lf, v, sg, aq, ak, av, sga, sgb = _proj(xp, gain_mix, lb, w_in_b, _tile(b * seq, 256))
    oa, s_fin = _hgrn(q, k, lf, v, sg, hg_gain, s_meta, b, HG_CHUNK, _tile(seq, 512))
    ob = _swa_prompt(sinks, aq, ak, av, km, vm, b)
    y_prompt = _mix_and_ffn(xp, oa, ob, sga, sgb, wts).reshape(b, seq, D_MODEL)
    akr = ak.reshape(b, seq, AT_KV_HEADS, AT_HEAD_DIM)
    avr = av.reshape(b, seq, AT_KV_HEADS, AT_HEAD_DIM)
    kmr = jnp.broadcast_to(km.reshape(1, N_META, AT_KV_HEADS, AT_HEAD_DIM), (b, N_META, AT_KV_HEADS, AT_HEAD_DIM))
    vmr = jnp.broadcast_to(vm.reshape(1, N_META, AT_KV_HEADS, AT_HEAD_DIM), (b, N_META, AT_KV_HEADS, AT_HEAD_DIM))

    xs = x_sample.reshape(db * ds, D_MODEL)
    q, k, lf, v, sg, aq, aks, avs, sga, sgb = _proj(xs, gain_mix, lb, w_in_b, _tile(db * ds, 256))
    cs = SUBLANES
    pad = lambda a: jnp.pad(a.reshape(db, ds, HG_WIDTH), ((0, 0), (0, cs - ds), (0, 0))).reshape(db * cs, HG_WIDTH)
    oa, s_new = _hgrn(pad(q), pad(k), pad(lf), pad(v), pad(sg), hg_gain, state_hgrn[l].astype(F32), db, cs, cs)
    oa = oa.reshape(db, cs, HG_WIDTH)[:, :ds].reshape(db * ds, HG_WIDTH)
    ob = _swa_sample(sinks, aq.reshape(db, ds, AT_WIDTH), aks.reshape(db, ds, AT_KV_WIDTH),
                     avs.reshape(db, ds, AT_KV_WIDTH),
                     cache_win_k[l].reshape(db, WINDOW, AT_KV_WIDTH), cache_win_v[l].reshape(db, WINDOW, AT_KV_WIDTH),
                     cache_meta_k[l].reshape(db, N_META, AT_KV_WIDTH), cache_meta_v[l].reshape(db, N_META, AT_KV_WIDTH),
                     8).reshape(db * ds, AT_WIDTH)
    y_sample = _mix_and_ffn(xs, oa, ob, sga, sgb, wts).reshape(db, ds, D_MODEL)

    return (y_prompt, y_sample,
            akr[:, -WINDOW:][None], avr[:, -WINDOW:][None], kmr[None], vmr[None], s_fin[None],
            aks.reshape(1, db, ds, AT_KV_HEADS, AT_HEAD_DIM), avs.reshape(1, db, ds, AT_KV_HEADS, AT_HEAD_DIM),
            s_new[None])
```

```python
import functools
import math

import numpy as np
import jax
import jax.numpy as jnp
from jax import lax
from jax.experimental import pallas as pl
from jax.experimental.pallas import tpu as pltpu

F32 = jnp.float32
BF16 = jnp.bfloat16
EPS = 1e-6
D_MODEL = 1024
N_META = 16
HG_HEADS = 4
HG_DK = 128
HG_WIDTH = HG_HEADS * HG_DK
HG_CHUNK = 64
HG_CHUNK_UNROLL = 8
AT_HEADS = 8
AT_KV_HEADS = 2
AT_GROUP = AT_HEADS // AT_KV_HEADS
AT_HEAD_DIM = 64
AT_WIDTH = AT_HEADS * AT_HEAD_DIM
AT_KV_WIDTH = AT_KV_HEADS * AT_HEAD_DIM
AT_SCALE = AT_HEAD_DIM ** -0.5
WINDOW = 128
BLOCK = 128
PAST_LEN = 16384
PEER_HEADS = 8
PEER_NKEYS = 128
PEER_HALF = 128
PEER_TOPK = 16
PEER_PICKS = PEER_HEADS * PEER_TOPK
SUBLANES = 8
LANES = 128
PEER_OCTETS = PEER_PICKS // SUBLANES
PEER_SLOTS = 2
VMEM_LIMIT = 56 * 1024 * 1024

NT_DIMS = (((1,), (1,)), ((), ()))
TN_DIMS = (((0,), (0,)), ((), ()))


def _cparams(*sem):
    return pltpu.CompilerParams(dimension_semantics=sem, vmem_limit_bytes=VMEM_LIMIT)


def _tile(n, pref):
    return pref if n % pref == 0 else n


def _sigmoid(x):
    return 1.0 / (1.0 + jnp.exp(-x))


def _proj_kernel(x_ref, gain_ref, lb_ref, w_ref, q_ref, k_ref, lf_ref, v_ref, sg_ref,
                 aq_ref, ak_ref, av_ref, sga_ref, sgb_ref):
    x = x_ref[...]
    xn = x * lax.rsqrt(jnp.mean(x * x, axis=-1, keepdims=True) + EPS) * gain_ref[...]
    xb = xn.astype(BF16)

    def proj(a, b):
        return jnp.dot(xb, w_ref[:, a:b], preferred_element_type=F32)

    o = 0
    hq = proj(o, o + HG_WIDTH); o += HG_WIDTH
    q_ref[...] = hq * _sigmoid(hq)
    lb = lb_ref[...]
    f = lb + (1.0 - lb) * _sigmoid(proj(o, o + HG_WIDTH)); o += HG_WIDTH
    k_ref[...] = 1.0 - f
    lf_ref[...] = jnp.log(f)
    v_ref[...] = proj(o, o + HG_WIDTH); o += HG_WIDTH
    hg = proj(o, o + HG_WIDTH); o += HG_WIDTH
    sg_ref[...] = hg * _sigmoid(hg)
    aq_ref[...] = proj(o, o + AT_WIDTH) * AT_SCALE; o += AT_WIDTH
    ak_ref[...] = proj(o, o + AT_KV_WIDTH); o += AT_KV_WIDTH
    av_ref[...] = proj(o, o + AT_KV_WIDTH); o += AT_KV_WIDTH
    sga_ref[...] = _sigmoid(proj(o, o + D_MODEL)); o += D_MODEL
    sgb_ref[...] = _sigmoid(proj(o, o + D_MODEL))


def _proj(x, gain, lb, w_bf16, tm):
    n = x.shape[0]
    widths = (HG_WIDTH,) * 5 + (AT_WIDTH, AT_KV_WIDTH, AT_KV_WIDTH, D_MODEL, D_MODEL)
    row = lambda w: pl.BlockSpec((tm, w), lambda i: (i, 0))
    full = lambda a: pl.BlockSpec(a.shape, lambda i: (0,) * a.ndim)
    return pl.pallas_call(
        _proj_kernel,
        grid=(n // tm,),
        in_specs=[row(D_MODEL), full(gain), full(lb), full(w_bf16)],
        out_specs=[row(w) for w in widths],
        out_shape=[jax.ShapeDtypeStruct((n, w), F32) for w in widths],
        compiler_params=_cparams("parallel"),
        name="proj",
    )(x, gain, lb, w_bf16)


def _hgrn_consts(c):
    levels = int(round(math.log2(c)))
    t = np.arange(c)
    le = (t[None, :] <= t[:, None]).astype(np.float32)
    mats = [le]
    masks = []
    for l in range(levels):
        bs, half = 2 << l, 1 << l
        base = (t // bs) * bs
        bnd = base + half - 1
        mats.append(le - (t[None, :] <= bnd[:, None]).astype(np.float32))
        right = (t % bs) >= half
        masks.append((base[:, None] == base[None, :]) & right[:, None] & (~right)[None, :])
    masks.append(np.eye(c, dtype=bool))
    a = np.concatenate(mats, 0)
    a2 = np.concatenate([a, a], 1)
    return jnp.asarray(a2, BF16), jnp.asarray(np.stack(masks).astype(np.float32))


def _hgrn_kernel(a_ref, m_ref, q_ref, k_ref, lf_ref, v_ref, sg_ref, gain_ref, s0_ref, o_ref, sfin_ref, *st_ref,
                 c, nchunk):
    levels = int(round(math.log2(c)))
    i = pl.program_id(1)

    @pl.when(i == 0)
    def _():
        for h in range(HG_HEADS):
            st_ref[h][...] = s0_ref[0, h].T

    gain = gain_ref[...]

    def chunk(ci, carry):
        r0 = pl.multiple_of(ci * c, c)
        rows = pl.ds(r0, c)
        heads = range(HG_HEADS)
        cols = [slice(h * HG_DK, (h + 1) * HG_DK) for h in heads]
        lf = lf_ref[rows, :]
        hi = lf.astype(BF16)
        lo = (lf - hi.astype(F32)).astype(BF16)
        r_all = jnp.dot(a_ref[...], jnp.concatenate([hi, lo], axis=0), preferred_element_type=F32)
        q = [q_ref[rows, cs] for cs in cols]
        k = [k_ref[rows, cs] for cs in cols]
        vb = [v_ref[rows, cs].astype(BF16) for cs in cols]
        st = [st_ref[h][...] for h in heads]
        r = [r_all[:, cs] for cs in cols]
        cum = [r[h][0:c] for h in heads]
        o = [lax.dot_general((q[h] * jnp.exp(cum[h])).astype(BF16), st[h].astype(BF16), NT_DIMS,
                             preferred_element_type=F32) for h in heads]
        att = [m_ref[levels] * lax.dot_general(q[h].astype(BF16), k[h].astype(BF16), NT_DIMS,
                                               preferred_element_type=F32) for h in heads]
        for l in range(levels):
            for h in heads:
                e = r[h][(l + 1) * c:(l + 2) * c]
                aq = (q[h] * jnp.exp(jnp.minimum(e, 0.0))).astype(BF16)
                ak = (k[h] * jnp.exp(jnp.minimum(-e, 0.0))).astype(BF16)
                att[h] = att[h] + m_ref[l] * lax.dot_general(aq, ak, NT_DIMS, preferred_element_type=F32)
        o = [o[h] + jnp.dot(att[h].astype(BF16), vb[h], preferred_element_type=F32) for h in heads]
        for h in heads:
            last = cum[h][c - 1:c]
            kdec = (k[h] * jnp.exp(last - cum[h])).astype(BF16)
            st_ref[h][...] = st[h] * jnp.exp(last) + lax.dot_general(vb[h], kdec, TN_DIMS,
                                                                    preferred_element_type=F32)
        for h in heads:
            on = o[h] * lax.rsqrt(jnp.mean(o[h] * o[h], axis=-1, keepdims=True) + EPS) * gain
            o_ref[rows, cols[h]] = on * sg_ref[rows, cols[h]]
        return carry

    unroll = math.gcd(nchunk, HG_CHUNK_UNROLL)

    def chunks(cj, carry):
        for u in range(unroll):
            chunk(cj * unroll + u, carry)
        return carry

    lax.fori_loop(0, nchunk // unroll, chunks, 0)

    @pl.when(i == pl.num_programs(1) - 1)
    def _():
        for h in range(HG_HEADS):
            sfin_ref[0, h] = st_ref[h][...].T


def _hgrn(q, k, lf, v, sg, gain, s0, batch, c, ct):
    n = q.shape[0]
    t = n // batch
    steps = t // ct
    a2, masks = _hgrn_consts(c)
    row = pl.BlockSpec((ct, HG_WIDTH), lambda b, i: (b * steps + i, 0))
    full = lambda a: pl.BlockSpec(a.shape, lambda b, i: (0,) * a.ndim)
    shared = s0.shape[0] == 1
    sspec = pl.BlockSpec((1, HG_HEADS, HG_DK, HG_DK), (lambda b, i: (0, 0, 0, 0)) if shared else (lambda b, i: (b, 0, 0, 0)))
    return pl.pallas_call(
        functools.partial(_hgrn_kernel, c=c, nchunk=ct // c),
        grid=(batch, steps),
        in_specs=[full(a2), full(masks), row, row, row, row, row, full(gain), sspec],
        out_specs=[row, pl.BlockSpec((1, HG_HEADS, HG_DK, HG_DK), lambda b, i: (b, 0, 0, 0))],
        out_shape=[jax.ShapeDtypeStruct((n, HG_WIDTH), F32),
                   jax.ShapeDtypeStruct((batch, HG_HEADS, HG_DK, HG_DK), F32)],
        scratch_shapes=[pltpu.VMEM((HG_DK, HG_DK), F32) for _ in range(HG_HEADS)],
        compiler_params=_cparams("parallel", "arbitrary"),
        name="hgrn",
    )(a2, masks, q, k, lf, v, sg, gain, s0)


def _swa_prompt_kernel(sink_ref, q_ref, kc_ref, kp_ref, vc_ref, vp_ref, km_ref, vm_ref, o_ref):
    n = pl.program_id(1)
    rows = AT_GROUP * BLOCK
    ri = lax.broadcasted_iota(jnp.int32, (rows, 2 * BLOCK), 0)
    sj = lax.broadcasted_iota(jnp.int32, (rows, 2 * BLOCK), 1)
    dist = ri % BLOCK + BLOCK - sj
    valid = (dist >= 0) & (dist <= WINDOW) & ((sj >= BLOCK) | (n > 0))
    distf = dist.astype(F32)
    gi = lax.broadcasted_iota(jnp.int32, (rows, 1), 0) // BLOCK
    for kv in range(AT_KV_HEADS):
        ks = slice(kv * AT_HEAD_DIM, (kv + 1) * AT_HEAD_DIM)
        kband = jnp.concatenate([kp_ref[:, ks], kc_ref[:, ks]], axis=0).astype(BF16)
        vband = jnp.concatenate([vp_ref[:, ks], vc_ref[:, ks]], axis=0).astype(BF16)
        kmeta = km_ref[:, ks].astype(BF16)
        vmeta = vm_ref[:, ks].astype(BF16)
        heads = [kv * AT_GROUP + g for g in range(AT_GROUP)]
        slope = jnp.exp2(-(gi + (kv * AT_GROUP + 1)).astype(F32) * (8.0 / AT_HEADS))
        sink = jnp.zeros((rows, 1), F32)
        for g, h in enumerate(heads):
            sink = jnp.where(gi == g, sink_ref[h], sink)
        qs = jnp.concatenate([q_ref[:, h * AT_HEAD_DIM:(h + 1) * AT_HEAD_DIM] for h in heads],
                             axis=0).astype(BF16)
        lband = lax.dot_general(qs, kband, NT_DIMS, preferred_element_type=F32) - slope * distf
        lband = jnp.where(valid, lband, -jnp.inf)
        lmeta = lax.dot_general(qs, kmeta, NT_DIMS, preferred_element_type=F32)
        m = jnp.maximum(jnp.maximum(jnp.max(lband, axis=-1, keepdims=True),
                                    jnp.max(lmeta, axis=-1, keepdims=True)), sink)
        eb = jnp.exp(lband - m)
        em = jnp.exp(lmeta - m)
        den = (jnp.sum(eb, axis=-1, keepdims=True) + jnp.sum(em, axis=-1, keepdims=True)
               + jnp.exp(sink - m))
        o = (jnp.dot(eb.astype(BF16), vband, preferred_element_type=F32)
             + jnp.dot(em.astype(BF16), vmeta, preferred_element_type=F32)) / den
        for g, h in enumerate(heads):
            o_ref[:, h * AT_HEAD_DIM:(h + 1) * AT_HEAD_DIM] = o[g * BLOCK:(g + 1) * BLOCK]


def _swa_prompt(sinks, aq, ak, av, km, vm, batch):
    n = aq.shape[0]
    nb = n // batch // BLOCK
    cur = lambda w: pl.BlockSpec((BLOCK, w), lambda b, i: (b * nb + i, 0))
    prev = lambda w: pl.BlockSpec((BLOCK, w), lambda b, i: (b * nb + jnp.maximum(i - 1, 0), 0))
    meta = pl.BlockSpec((N_META, AT_KV_WIDTH), lambda b, i: (0, 0))
    return pl.pallas_call(
        _swa_prompt_kernel,
        grid=(batch, nb),
        in_specs=[pl.BlockSpec(memory_space=pltpu.SMEM), cur(AT_WIDTH), cur(AT_KV_WIDTH), prev(AT_KV_WIDTH),
                  cur(AT_KV_WIDTH), prev(AT_KV_WIDTH), meta, meta],
        out_specs=cur(AT_WIDTH),
        out_shape=jax.ShapeDtypeStruct((n, AT_WIDTH), F32),
        compiler_params=_cparams("parallel", "parallel"),
        name="swa_prompt",
    )(sinks, aq, ak, ak, av, av, km, vm)


def _swa_sample_kernel(sink_ref, q_ref, kn_ref, vn_ref, kw_ref, vw_ref, km_ref, vm_ref, o_ref, *, bg, ds):
    rows = AT_GROUP * ds
    nk = WINDOW + ds
    ri = lax.broadcasted_iota(jnp.int32, (rows, nk), 0)
    sj = lax.broadcasted_iota(jnp.int32, (rows, nk), 1)
    qpos = PAST_LEN + ri % ds
    kpos = PAST_LEN - WINDOW + sj
    dist = qpos - kpos
    valid = (dist >= 0) & (dist <= WINDOW) & (kpos >= N_META)
    distf = dist.astype(F32)
    gi = lax.broadcasted_iota(jnp.int32, (rows, 1), 0) // ds
    for kv in range(AT_KV_HEADS):
        ks = slice(kv * AT_HEAD_DIM, (kv + 1) * AT_HEAD_DIM)
        slope = jnp.exp2(-(gi + (kv * AT_GROUP + 1)).astype(F32) * (8.0 / AT_HEADS))
        sink = jnp.zeros((rows, 1), F32)
        for g in range(AT_GROUP):
            sink = jnp.where(gi == g, sink_ref[kv * AT_GROUP + g], sink)
        for b in range(bg):
            kall = jnp.concatenate([kw_ref[b, :, ks], kn_ref[b, :, ks]], axis=0).astype(BF16)
            vall = jnp.concatenate([vw_ref[b, :, ks], vn_ref[b, :, ks]], axis=0).astype(BF16)
            kmeta = km_ref[b, :, ks].astype(BF16)
            vmeta = vm_ref[b, :, ks].astype(BF16)
            qs = jnp.concatenate(
                [q_ref[b, :, (kv * AT_GROUP + g) * AT_HEAD_DIM:(kv * AT_GROUP + g + 1) * AT_HEAD_DIM]
                 for g in range(AT_GROUP)], axis=0).astype(BF16)
            lw = lax.dot_general(qs, kall, NT_DIMS, preferred_element_type=F32) - slope * distf
            lw = jnp.where(valid, lw, -jnp.inf)
            lm = lax.dot_general(qs, kmeta, NT_DIMS, preferred_element_type=F32)
            m = jnp.maximum(jnp.maximum(jnp.max(lw, axis=-1, keepdims=True),
                                        jnp.max(lm, axis=-1, keepdims=True)), sink)
            ew = jnp.exp(lw - m)
            em = jnp.exp(lm - m)
            den = (jnp.sum(ew, axis=-1, keepdims=True) + jnp.sum(em, axis=-1, keepdims=True)
                   + jnp.exp(sink - m))
            o = (jnp.dot(ew.astype(BF16), vall, preferred_element_type=F32)
                 + jnp.dot(em.astype(BF16), vmeta, preferred_element_type=F32)) / den
            for g in range(AT_GROUP):
                h = kv * AT_GROUP + g
                o_ref[b, :, h * AT_HEAD_DIM:(h + 1) * AT_HEAD_DIM] = o[g * ds:(g + 1) * ds]


def _swa_sample(sinks, aq, ak, av, win_k, win_v, meta_k, meta_v, bg):
    b, ds = aq.shape[:2]
    spec = lambda a: pl.BlockSpec((bg,) + a.shape[1:], lambda i: (i, 0, 0))
    args = (aq, ak, av, win_k, win_v, meta_k, meta_v)
    return pl.pallas_call(
        functools.partial(_swa_sample_kernel, bg=bg, ds=ds),
        grid=(b // bg,),
        in_specs=[pl.BlockSpec(memory_space=pltpu.SMEM)] + [spec(a) for a in args],
        out_specs=spec(aq),
        out_shape=jax.ShapeDtypeStruct(aq.shape, F32),
        compiler_params=_cparams("parallel"),
        name="swa_sample",
    )(sinks, *args)


def _merge_kernel(x_ref, oa_ref, ob_ref, sga_ref, sgb_ref, wpa_ref, wpb_ref, wo_ref, gain_ref, wq_ref, sk_ref,
                  h2_ref, xn_ref, sc_ref):
    pa = jnp.dot(oa_ref[...].astype(BF16), wpa_ref[...], preferred_element_type=F32)
    pb = jnp.dot(ob_ref[...].astype(BF16), wpb_ref[...], preferred_element_type=F32)
    m = sga_ref[...] * pa + sgb_ref[...] * pb
    h2 = x_ref[...] + jnp.dot(m.astype(BF16), wo_ref[...], preferred_element_type=F32)
    h2_ref[...] = h2
    xn = h2 * lax.rsqrt(jnp.mean(h2 * h2, axis=-1, keepdims=True) + EPS) * gain_ref[...]
    xn_ref[...] = xn
    qp = jnp.dot(xn.astype(BF16), wq_ref[...], preferred_element_type=F32).astype(BF16)
    for hc in range(2 * PEER_HEADS):
        sc_ref[hc] = lax.dot_general(sk_ref[hc], qp[:, hc * PEER_HALF:(hc + 1) * PEER_HALF], NT_DIMS,
                                     preferred_element_type=F32)


def _merge(x, oa, ob, sga, sgb, wpa, wpb, wo, gain, wq, sk, tm):
    n = x.shape[0]
    row = lambda w: pl.BlockSpec((tm, w), lambda i: (i, 0))
    full = lambda a: pl.BlockSpec(a.shape, lambda i: (0,) * a.ndim)
    return pl.pallas_call(
        _merge_kernel,
        grid=(n // tm,),
        in_specs=[row(D_MODEL), row(HG_WIDTH), row(AT_WIDTH), row(D_MODEL), row(D_MODEL),
                  full(wpa), full(wpb), full(wo), full(gain), full(wq), full(sk)],
        out_specs=[row(D_MODEL), row(D_MODEL),
                   pl.BlockSpec((2 * PEER_HEADS, PEER_NKEYS, tm), lambda i: (0, 0, i))],
        out_shape=[jax.ShapeDtypeStruct((n, D_MODEL), F32), jax.ShapeDtypeStruct((n, D_MODEL), F32),
                   jax.ShapeDtypeStruct((2 * PEER_HEADS, PEER_NKEYS, n), F32)],
        compiler_params=_cparams("parallel"),
        name="merge",
    )(x, oa, ob, sga, sgb, wpa, wpb, wo, gain, wq, sk)


def _top16(arrays):
    arrays = list(arrays)
    t = arrays[0].shape[1]
    io16 = lax.broadcasted_iota(jnp.int32, (PEER_TOPK, t), 0)
    ios = [lax.broadcasted_iota(jnp.int32, s.shape, 0).astype(F32) for s in arrays]
    vals = [jnp.zeros((PEER_TOPK, t), F32) for _ in arrays]
    idxs = [jnp.zeros((PEER_TOPK, t), F32) for _ in arrays]
    for j in range(PEER_TOPK):
        for k, (s, io) in enumerate(zip(arrays, ios)):
            m = jnp.max(s, axis=0, keepdims=True)
            idx = jnp.min(jnp.where(s == m, io, float(s.shape[0])), axis=0, keepdims=True)
            vals[k] = jnp.where(io16 == j, m, vals[k])
            idxs[k] = jnp.where(io16 == j, idx, idxs[k])
            arrays[k] = jnp.where(io == idx, -jnp.inf, s)
    return vals, [i.astype(jnp.int32) for i in idxs]


_CAND_A0, _CAND_MID, _CAND_ROWS = PEER_TOPK, PEER_TOPK + 7 * SUBLANES, PEER_TOPK + 8 * SUBLANES


def _candidates(v1, v2):
    t = v1.shape[1]
    io8 = lax.broadcasted_iota(jnp.int32, (SUBLANES, t), 0)
    pieces = [v1[0:1] + v2]
    for a in range(1, SUBLANES):
        keep = PEER_TOPK // (a + 1)
        p = v1[a:a + 1] + v2[0:SUBLANES]
        pieces.append(p if keep >= SUBLANES else jnp.where(io8 < keep, p, -jnp.inf))
    pieces.append(v1[SUBLANES:] + v2[0:1])
    return jnp.concatenate(pieces, axis=0)


def _candidate_ab(pos):
    mid = pos - _CAND_A0
    a = jnp.where(pos < _CAND_A0, 0, jnp.where(pos < _CAND_MID, (mid >> 3) + 1, pos - (_CAND_MID - SUBLANES)))
    b = jnp.where(pos < _CAND_A0, pos, jnp.where(pos < _CAND_MID, mid & (SUBLANES - 1), 0))
    return a, b


def _topk_kernel(s_ref, eidx_ref, gate_ref):
    for h0 in range(0, PEER_HEADS, 2):
        heads = (h0, h0 + 1)
        sub = [_top16([s_ref[2 * h], s_ref[2 * h + 1]]) for h in heads]
        ctops, cposs = _top16([_candidates(vals[0], vals[1]) for vals, _ in sub])
        for h, (_, (i1, i2)), ctop, cpos in zip(heads, sub, ctops, cposs):
            a, b = _candidate_ab(cpos)
            e1 = jnp.zeros_like(a)
            e2 = jnp.zeros_like(b)
            for r in range(PEER_TOPK):
                e1 = jnp.where(a == r, i1[r:r + 1], e1)
                e2 = jnp.where(b == r, i2[r:r + 1], e2)
            ex = jnp.exp(ctop - ctop[0:1])
            sl = slice(h * PEER_TOPK, (h + 1) * PEER_TOPK)
            gate_ref[sl, :] = ex / jnp.sum(ex, axis=0, keepdims=True)
            eidx_ref[sl, :] = e1 * PEER_NKEYS + e2


def _topk(scores_t, tk):
    n = scores_t.shape[2]
    return pl.pallas_call(
        _topk_kernel,
        grid=(n // tk,),
        in_specs=[pl.BlockSpec((2 * PEER_HEADS, PEER_NKEYS, tk), lambda i: (0, 0, i))],
        out_specs=[pl.BlockSpec((PEER_PICKS, tk), lambda i: (0, i))] * 2,
        out_shape=[jax.ShapeDtypeStruct((PEER_PICKS, n), jnp.int32),
                   jax.ShapeDtypeStruct((PEER_PICKS, n), F32)],
        compiler_params=_cparams("parallel"),
        name="topk",
    )(scores_t)


ROW_PARTS = SUBLANES // 2


def _pack_rows_kernel(x_ref, o_ref):
    x = x_ref[...]
    for s in range(ROW_PARTS):
        even = x[:, (2 * s) * LANES:(2 * s + 1) * LANES]
        odd = x[:, (2 * s + 1) * LANES:(2 * s + 2) * LANES]
        words = pltpu.pack_elementwise([even, odd], packed_dtype=BF16)
        o_ref[:, s * LANES:(s + 1) * LANES] = pltpu.bitcast(words, jnp.int32)


def _table_rows(tab):
    e = tab.shape[0]
    tm = _tile(e, 256)
    words = pl.pallas_call(
        _pack_rows_kernel,
        grid=(e // tm,),
        in_specs=[pl.BlockSpec((tm, D_MODEL), lambda i: (i, 0))],
        out_specs=pl.BlockSpec((tm, ROW_PARTS * LANES), lambda i: (i, 0)),
        out_shape=jax.ShapeDtypeStruct((e, ROW_PARTS * LANES), jnp.int32),
        compiler_params=_cparams("parallel"),
        name="pack_rows",
    )(tab)
    return words.reshape(e * ROW_PARTS, LANES)


def _load_row(tab_ref, offset):
    words = tab_ref[pl.ds(pl.multiple_of(offset, ROW_PARTS), ROW_PARTS), :]
    return pltpu.bitcast(words, BF16).astype(F32)


_BITREV = tuple(int(format(i, "03b")[::-1], 2) for i in range(SUBLANES))


def _fold_sublanes(a, b, half):
    s = lax.broadcasted_iota(jnp.int32, (SUBLANES, LANES), 0)
    low = (s & half) == 0
    if 2 * half < SUBLANES:
        b = pltpu.roll(b, half, axis=0)
    return jnp.where(low, a, b) + pltpu.roll(jnp.where(low, b, a), SUBLANES - half, axis=0)


def _split_indices(eidx_t):
    n = eidx_t.shape[1]
    return eidx_t.reshape(PEER_OCTETS, SUBLANES, n).transpose(1, 2, 0).reshape(SUBLANES, n * PEER_OCTETS)


def _for_each_index_slot(idx_hbm, sm_refs, sem, tb, process):
    step = pl.program_id(0)
    words = tb * PEER_OCTETS

    def copies(block, slot):
        return [pltpu.make_async_copy(idx_hbm.at[i, pl.ds(block * words, words)], sm_refs[slot][i], sem.at[slot, i])
                for i in range(SUBLANES)]

    def start(block, slot):
        for c in copies(block, slot):
            c.start()

    @pl.when(step == 0)
    def _():
        start(0, 0)

    for slot in range(PEER_SLOTS):
        block = step * PEER_SLOTS + slot
        for c in copies(block, slot):
            c.wait()
        if slot + 1 < PEER_SLOTS:
            start(block + 1, slot + 1)
        else:
            @pl.when(step + 1 < pl.num_programs(0))
            def _():
                start(block + 1, 0)
        process(slot)


def _index_scratch(tb):
    return ([pltpu.SMEM((tb * PEER_OCTETS,), jnp.int32) for _ in range(PEER_SLOTS * SUBLANES)]
            + [pltpu.SemaphoreType.DMA((PEER_SLOTS, SUBLANES))])


def _index_refs(refs):
    return [refs[s * SUBLANES:(s + 1) * SUBLANES] for s in range(PEER_SLOTS)]


def _peer_a_kernel(idx_hbm, x_ref, gate_ref, tab_ref, w_ref, *scratch, tb):
    sm_refs, sem, q_ref, h_ref = _index_refs(scratch), scratch[-3], scratch[-2], scratch[-1]
    lane = lax.broadcasted_iota(jnp.int32, (SUBLANES, tb), 1)

    def place(sums, t):
        h = h_ref[...]
        for octet, col in enumerate(sums):
            rows = slice(octet * SUBLANES, (octet + 1) * SUBLANES)
            h_ref[rows, :] = jnp.where(lane == t, col, h[rows, :])

    def process(slot):
        t0 = slot * tb
        sm = sm_refs[slot]
        h_ref[...] = jnp.zeros_like(h_ref)
        q_ref[...] = jnp.zeros_like(q_ref)

        def lane_sums():
            q = q_ref[...]
            return [jnp.sum(q[o * SUBLANES:(o + 1) * SUBLANES, :], axis=1, keepdims=True) for o in range(PEER_OCTETS)]

        def tok(t, carry):
            sums = lane_sums()
            xv = x_ref[t0 + t]
            for octet in range(PEER_OCTETS):
                base = t * PEER_OCTETS + octet
                tiles = [_load_row(tab_ref, sm[_BITREV[i]][base]) * xv for i in range(SUBLANES)]
                half = SUBLANES // 2
                while len(tiles) > 1:
                    tiles = [_fold_sublanes(tiles[i], tiles[i + 1], half) for i in range(0, len(tiles), 2)]
                    half //= 2
                q_ref[octet * SUBLANES:(octet + 1) * SUBLANES, :] = tiles[0]
            place(sums, t - 1)
            return carry

        lax.fori_loop(0, tb, tok, 0)
        place(lane_sums(), tb - 1)
        h = h_ref[...]
        w_ref[:, t0:t0 + tb] = gate_ref[:, t0:t0 + tb] * (0.5 * h * (1.0 + lax.erf(h * math.sqrt(0.5))))

    _for_each_index_slot(idx_hbm, sm_refs, sem, tb, process)


def _peer_a(eidx_split, xn3, gate_t, tab, tb):
    n = xn3.shape[0]
    span = PEER_SLOTS * tb
    return pl.pallas_call(
        functools.partial(_peer_a_kernel, tb=tb),
        grid=(n // span,),
        in_specs=[pl.BlockSpec(memory_space=pl.ANY),
                  pl.BlockSpec((span, SUBLANES, LANES), lambda i: (i, 0, 0)),
                  pl.BlockSpec((PEER_PICKS, span), lambda i: (0, i)),
                  pl.BlockSpec(tab.shape, lambda i: (0, 0), pipeline_mode=pl.Buffered(1))],
        out_specs=pl.BlockSpec((PEER_PICKS, span), lambda i: (0, i)),
        out_shape=jax.ShapeDtypeStruct((PEER_PICKS, n), F32),
        scratch_shapes=_index_scratch(tb) + [pltpu.VMEM((PEER_PICKS, LANES), F32), pltpu.VMEM((PEER_PICKS, tb), F32)],
        compiler_params=_cparams("arbitrary"),
        name="peer_a",
    )(eidx_split, xn3, gate_t, tab)


def _peer_b_kernel(idx_hbm, wt_ref, h2_ref, gain_ref, tab_ref, y_ref, *scratch, tb):
    sm_refs, sem, wb_refs = _index_refs(scratch), scratch[-3], scratch[-2:]
    nacc = 2

    def process(slot):
        t0 = slot * tb
        sm = sm_refs[slot]
        wt = wt_ref[:, t0:t0 + tb]

        def spread(wb_ref, t):
            wb_ref[...] = jnp.take_along_axis(wt, jnp.full((PEER_PICKS, LANES), t, jnp.int32), axis=1)

        spread(wb_refs[0], 0)

        def pair(tp, carry):
            for par in range(2):
                t = 2 * tp + par
                wb_ref = wb_refs[par]
                spread(wb_refs[1 - par], jnp.minimum(t + 1, tb - 1))
                acc = [jnp.zeros((SUBLANES, LANES), F32) for _ in range(nacc)]
                for octet in range(PEER_OCTETS):
                    base = t * PEER_OCTETS + octet
                    wg = wb_ref[octet * SUBLANES:(octet + 1) * SUBLANES, :]
                    for i in range(SUBLANES):
                        acc[i % nacc] = acc[i % nacc] + wg[i:i + 1, :] * _load_row(tab_ref, sm[i][base])
                total = acc[0]
                for a in acc[1:]:
                    total = total + a
                y_ref[t0 + t] = h2_ref[t0 + t] + total
            return carry

        lax.fori_loop(0, tb // 2, pair, 0)

    _for_each_index_slot(idx_hbm, sm_refs, sem, tb, process)

    y = y_ref[...]
    ms = jnp.sum(jnp.sum(y * y, axis=2, keepdims=True), axis=1, keepdims=True) * (1.0 / D_MODEL)
    y_ref[...] = y * lax.rsqrt(ms + EPS) * gain_ref[...][None]


def _peer_b(eidx_split, w_t, h23, gain3, tab, tb):
    n = h23.shape[0]
    span = PEER_SLOTS * tb
    tok3 = pl.BlockSpec((span, SUBLANES, LANES), lambda i: (i, 0, 0))
    return pl.pallas_call(
        functools.partial(_peer_b_kernel, tb=tb),
        grid=(n // span,),
        in_specs=[pl.BlockSpec(memory_space=pl.ANY),
                  pl.BlockSpec((PEER_PICKS, span), lambda i: (0, i)),
                  tok3, pl.BlockSpec(gain3.shape, lambda i: (0, 0)),
                  pl.BlockSpec(tab.shape, lambda i: (0, 0), pipeline_mode=pl.Buffered(1))],
        out_specs=tok3,
        out_shape=jax.ShapeDtypeStruct(h23.shape, F32),
        scratch_shapes=_index_scratch(tb) + [pltpu.VMEM((PEER_PICKS, LANES), F32) for _ in range(2)],
        compiler_params=_cparams("arbitrary"),
        name="peer_b",
    )(eidx_split, w_t, h23, gain3, tab)


def _mix_and_ffn(x, oa, ob, sga, sgb, wts):
    n = x.shape[0]
    tm, tk, tb = _tile(n, 256), _tile(n, LANES), LANES
    assert n % (PEER_SLOTS * tb) == 0, "PEER kernels take whole groups of token blocks"
    h2, xn, scores_t = _merge(x, oa, ob, sga, sgb, wts["wpa"], wts["wpb"], wts["wo"], wts["gain_ffn"],
                              wts["wq"], wts["sk"], tm)
    eidx_t, gate_t = _topk(scores_t, tk)
    eidx_split = _split_indices(eidx_t * ROW_PARTS)
    w_t = _peer_a(eidx_split, xn.reshape(n, SUBLANES, LANES), gate_t, wts["u"], tb)
    y3 = _peer_b(eidx_split, w_t, h2.reshape(n, SUBLANES, LANES), wts["gain_final"], wts["v"], tb)
    return y3.reshape(n, D_MODEL)


def kernel(x_prompt, x_sample, cache_win_k, cache_win_v, cache_meta_k, cache_meta_v, state_hgrn, meta_tokens,
           norm_mix, w_in, hg_lb, hg_norm, attn_sinks, w_pa, w_pb, w_o, norm_ffn, peer_wq, peer_subkeys,
           peer_u, peer_v, norm_final):
    depth = w_in.shape[0]
    assert depth == 1, "single-layer step only"
    b, seq, _ = x_prompt.shape
    db, ds, _ = x_sample.shape
    l = 0
    lb = jnp.cumsum(jax.nn.softmax(hg_lb.astype(F32), axis=0), axis=0)[l][None, :]
    gain_mix = norm_mix[l][None, :]
    w_in_b = w_in[l].astype(BF16)
    hg_gain = hg_norm[l][None, :]
    sinks = attn_sinks[l].astype(F32)
    wts = dict(
        wpa=w_pa[l].astype(BF16), wpb=w_pb[l].astype(BF16), wo=w_o[l].astype(BF16),
        gain_ffn=norm_ffn[l][None, :], wq=peer_wq[l].astype(BF16),
        sk=peer_subkeys[l].reshape(2 * PEER_HEADS, PEER_NKEYS, PEER_HALF).astype(BF16),
        u=_table_rows(peer_u[l]), v=_table_rows(peer_v[l]),
        gain_final=norm_final.reshape(SUBLANES, LANES),
    )

    mq, mk, mlf, mv, msg, _, km, vm, _, _ = _proj(meta_tokens, gain_mix, lb, w_in_b, N_META)
    zero_state = jnp.zeros((1, HG_HEADS, HG_DK, HG_DK), F32)
    _, s_meta = _hgrn(mq, mk, mlf, mv, msg, hg_gain, zero_state, 1, N_META, N_META)

    xp = x_prompt.reshape(b * seq, D_MODEL)
    q, k, lf, v, sg, aq, ak, av, sga, sgb = _proj(xp, gain_mix, lb, w_in_b, _tile(b * seq, 256))
    oa, s_fin = _hgrn(q, k, lf, v, sg, hg_gain, s_meta, b, HG_CHUNK, _tile(seq, 512))
    ob = _swa_prompt(sinks, aq, ak, av, km, vm, b)
    y_prompt = _mix_and_ffn(xp, oa, ob, sga, sgb, wts).reshape(b, seq, D_MODEL)
    akr = ak.reshape(b, seq, AT_KV_HEADS, AT_HEAD_DIM)
    avr = av.reshape(b, seq, AT_KV_HEADS, AT_HEAD_DIM)
    kmr = jnp.broadcast_to(km.reshape(1, N_META, AT_KV_HEADS, AT_HEAD_DIM), (b, N_META, AT_KV_HEADS, AT_HEAD_DIM))
    vmr = jnp.broadcast_to(vm.reshape(1, N_META, AT_KV_HEADS, AT_HEAD_DIM), (b, N_META, AT_KV_HEADS, AT_HEAD_DIM))

    xs = x_sample.reshape(db * ds, D_MODEL)
    q, k, lf, v, sg, aq, aks, avs, sga, sgb = _proj(xs, gain_mix, lb, w_in_b, _tile(db * ds, 256))
    cs = SUBLANES
    pad = lambda a: jnp.pad(a.reshape(db, ds, HG_WIDTH), ((0, 0), (0, cs - ds), (0, 0))).reshape(db * cs, HG_WIDTH)
    oa, s_new = _hgrn(pad(q), pad(k), pad(lf), pad(v), pad(sg), hg_gain, state_hgrn[l].astype(F32), db, cs, cs)
    oa = oa.reshape(db, cs, HG_WIDTH)[:, :ds].reshape(db * ds, HG_WIDTH)
    ob = _swa_sample(sinks, aq.reshape(db, ds, AT_WIDTH), aks.reshape(db, ds, AT_KV_WIDTH),
                     avs.reshape(db, ds, AT_KV_WIDTH),
                     cache_win_k[l].reshape(db, WINDOW, AT_KV_WIDTH), cache_win_v[l].reshape(db, WINDOW, AT_KV_WIDTH),
                     cache_meta_k[l].reshape(db, N_META, AT_KV_WIDTH), cache_meta_v[l].reshape(db, N_META, AT_KV_WIDTH),
                     8).reshape(db * ds, AT_WIDTH)
    y_sample = _mix_and_ffn(xs, oa, ob, sga, sgb, wts).reshape(db, ds, D_MODEL)

    return (y_prompt, y_sample,
            akr[:, -WINDOW:][None], avr[:, -WINDOW:][None], kmr[None], vmr[None], s_fin[None],
            aks.reshape(1, db, ds, AT_KV_HEADS, AT_HEAD_DIM), avs.reshape(1, db, ds, AT_KV_HEADS, AT_HEAD_DIM),
            s_new[None])
```

```python
import functools
import math

import numpy as np
import jax
import jax.numpy as jnp
from jax import lax
from jax.experimental import pallas as pl
from jax.experimental.pallas import tpu as pltpu

F32 = jnp.float32
BF16 = jnp.bfloat16
EPS = 1e-6
D_MODEL = 1024
N_META = 16
HG_HEADS = 4
HG_DK = 128
HG_WIDTH = HG_HEADS * HG_DK
HG_CHUNK = 64
HG_CHUNK_UNROLL = 8
AT_HEADS = 8
AT_KV_HEADS = 2
AT_GROUP = AT_HEADS // AT_KV_HEADS
AT_HEAD_DIM = 64
AT_WIDTH = AT_HEADS * AT_HEAD_DIM
AT_KV_WIDTH = AT_KV_HEADS * AT_HEAD_DIM
AT_SCALE = AT_HEAD_DIM ** -0.5
WINDOW = 128
BLOCK = 128
PAST_LEN = 16384
PEER_HEADS = 8
PEER_NKEYS = 128
PEER_HALF = 128
PEER_TOPK = 16
PEER_PICKS = PEER_HEADS * PEER_TOPK
SUBLANES = 8
LANES = 128
PEER_OCTETS = PEER_PICKS // SUBLANES
PEER_SLOTS = 2
VMEM_LIMIT = 56 * 1024 * 1024

NT_DIMS = (((1,), (1,)), ((), ()))
TN_DIMS = (((0,), (0,)), ((), ()))


def _cparams(*sem):
    return pltpu.CompilerParams(dimension_semantics=sem, vmem_limit_bytes=VMEM_LIMIT)


def _tile(n, pref):
    return pref if n % pref == 0 else n


def _sigmoid(x):
    return 1.0 / (1.0 + jnp.exp(-x))


def _proj_kernel(x_ref, gain_ref, lb_ref, w_ref, q_ref, k_ref, lf_ref, v_ref, sg_ref,
                 aq_ref, ak_ref, av_ref, sga_ref, sgb_ref):
    x = x_ref[...]
    xn = x * lax.rsqrt(jnp.mean(x * x, axis=-1, keepdims=True) + EPS) * gain_ref[...]
    xb = xn.astype(BF16)

    def proj(a, b):
        return jnp.dot(xb, w_ref[:, a:b], preferred_element_type=F32)

    o = 0
    hq = proj(o, o + HG_WIDTH); o += HG_WIDTH
    q_ref[...] = hq * _sigmoid(hq)
    lb = lb_ref[...]
    f = lb + (1.0 - lb) * _sigmoid(proj(o, o + HG_WIDTH)); o += HG_WIDTH
    k_ref[...] = 1.0 - f
    lf_ref[...] = jnp.log(f)
    v_ref[...] = proj(o, o + HG_WIDTH); o += HG_WIDTH
    hg = proj(o, o + HG_WIDTH); o += HG_WIDTH
    sg_ref[...] = hg * _sigmoid(hg)
    aq_ref[...] = proj(o, o + AT_WIDTH) * AT_SCALE; o += AT_WIDTH
    ak_ref[...] = proj(o, o + AT_KV_WIDTH); o += AT_KV_WIDTH
    av_ref[...] = proj(o, o + AT_KV_WIDTH); o += AT_KV_WIDTH
    sga_ref[...] = _sigmoid(proj(o, o + D_MODEL)); o += D_MODEL
    sgb_ref[...] = _sigmoid(proj(o, o + D_MODEL))


def _proj(x, gain, lb, w_bf16, tm):
    n = x.shape[0]
    widths = (HG_WIDTH,) * 5 + (AT_WIDTH, AT_KV_WIDTH, AT_KV_WIDTH, D_MODEL, D_MODEL)
    row = lambda w: pl.BlockSpec((tm, w), lambda i: (i, 0))
    full = lambda a: pl.BlockSpec(a.shape, lambda i: (0,) * a.ndim)
    return pl.pallas_call(
        _proj_kernel,
        grid=(n // tm,),
        in_specs=[row(D_MODEL), full(gain), full(lb), full(w_bf16)],
        out_specs=[row(w) for w in widths],
        out_shape=[jax.ShapeDtypeStruct((n, w), F32) for w in widths],
        compiler_params=_cparams("parallel"),
        name="proj",
    )(x, gain, lb, w_bf16)


def _hgrn_consts(c):
    levels = int(round(math.log2(c)))
    t = np.arange(c)
    le = (t[None, :] <= t[:, None]).astype(np.float32)
    mats = [le]
    masks = []
    for l in range(levels):
        bs, half = 2 << l, 1 << l
        base = (t // bs) * bs
        bnd = base + half - 1
        mats.append(le - (t[None, :] <= bnd[:, None]).astype(np.float32))
        right = (t % bs) >= half
        masks.append((base[:, None] == base[None, :]) & right[:, None] & (~right)[None, :])
    masks.append(np.eye(c, dtype=bool))
    a = np.concatenate(mats, 0)
    a2 = np.concatenate([a, a], 1)
    return jnp.asarray(a2, BF16), jnp.asarray(np.stack(masks).astype(np.float32))


def _hgrn_kernel(a_ref, m_ref, q_ref, k_ref, lf_ref, v_ref, sg_ref, gain_ref, s0_ref, o_ref, sfin_ref, *st_ref,
                 c, nchunk):
    levels = int(round(math.log2(c)))
    i = pl.program_id(1)

    @pl.when(i == 0)
    def _():
        for h in range(HG_HEADS):
            st_ref[h][...] = s0_ref[0, h].T

    gain = gain_ref[...]

    def chunk(ci, carry):
        r0 = pl.multiple_of(ci * c, c)
        rows = pl.ds(r0, c)
        heads = range(HG_HEADS)
        cols = [slice(h * HG_DK, (h + 1) * HG_DK) for h in heads]
        lf = lf_ref[rows, :]
        hi = lf.astype(BF16)
        lo = (lf - hi.astype(F32)).astype(BF16)
        r_all = jnp.dot(a_ref[...], jnp.concatenate([hi, lo], axis=0), preferred_element_type=F32)
        q = [q_ref[rows, cs] for cs in cols]
        k = [k_ref[rows, cs] for cs in cols]
        vb = [v_ref[rows, cs].astype(BF16) for cs in cols]
        st = [st_ref[h][...] for h in heads]
        r = [r_all[:, cs] for cs in cols]
        cum = [r[h][0:c] for h in heads]
        o = [lax.dot_general((q[h] * jnp.exp(cum[h])).astype(BF16), st[h].astype(BF16), NT_DIMS,
                             preferred_element_type=F32) for h in heads]
        att = [m_ref[levels] * lax.dot_general(q[h].astype(BF16), k[h].astype(BF16), NT_DIMS,
                                               preferred_element_type=F32) for h in heads]
        for l in range(levels):
            for h in heads:
                e = r[h][(l + 1) * c:(l + 2) * c]
                aq = (q[h] * jnp.exp(jnp.minimum(e, 0.0))).astype(BF16)
                ak = (k[h] * jnp.exp(jnp.minimum(-e, 0.0))).astype(BF16)
                att[h] = att[h] + m_ref[l] * lax.dot_general(aq, ak, NT_DIMS, preferred_element_type=F32)
        o = [o[h] + jnp.dot(att[h].astype(BF16), vb[h], preferred_element_type=F32) for h in heads]
        for h in heads:
            last = cum[h][c - 1:c]
            kdec = (k[h] * jnp.exp(last - cum[h])).astype(BF16)
            st_ref[h][...] = st[h] * jnp.exp(last) + lax.dot_general(vb[h], kdec, TN_DIMS,
                                                                    preferred_element_type=F32)
        for h in heads:
            on = o[h] * lax.rsqrt(jnp.mean(o[h] * o[h], axis=-1, keepdims=True) + EPS) * gain
            o_ref[rows, cols[h]] = on * sg_ref[rows, cols[h]]
        return carry

    unroll = math.gcd(nchunk, HG_CHUNK_UNROLL)

    def chunks(cj, carry):
        for u in range(unroll):
            chunk(cj * unroll + u, carry)
        return carry

    lax.fori_loop(0, nchunk // unroll, chunks, 0)

    @pl.when(i == pl.num_programs(1) - 1)
    def _():
        for h in range(HG_HEADS):
            sfin_ref[0, h] = st_ref[h][...].T


def _hgrn(q, k, lf, v, sg, gain, s0, batch, c, ct):
    n = q.shape[0]
    t = n // batch
    steps = t // ct
    a2, masks = _hgrn_consts(c)
    row = pl.BlockSpec((ct, HG_WIDTH), lambda b, i: (b * steps + i, 0))
    full = lambda a: pl.BlockSpec(a.shape, lambda b, i: (0,) * a.ndim)
    shared = s0.shape[0] == 1
    sspec = pl.BlockSpec((1, HG_HEADS, HG_DK, HG_DK), (lambda b, i: (0, 0, 0, 0)) if shared else (lambda b, i: (b, 0, 0, 0)))
    return pl.pallas_call(
        functools.partial(_hgrn_kernel, c=c, nchunk=ct // c),
        grid=(batch, steps),
        in_specs=[full(a2), full(masks), row, row, row, row, row, full(gain), sspec],
        out_specs=[row, pl.BlockSpec((1, HG_HEADS, HG_DK, HG_DK), lambda b, i: (b, 0, 0, 0))],
        out_shape=[jax.ShapeDtypeStruct((n, HG_WIDTH), F32),
                   jax.ShapeDtypeStruct((batch, HG_HEADS, HG_DK, HG_DK), F32)],
        scratch_shapes=[pltpu.VMEM((HG_DK, HG_DK), F32) for _ in range(HG_HEADS)],
        compiler_params=_cparams("parallel", "arbitrary"),
        name="hgrn",
    )(a2, masks, q, k, lf, v, sg, gain, s0)


def _swa_prompt_kernel(sink_ref, q_ref, kc_ref, kp_ref, vc_ref, vp_ref, km_ref, vm_ref, o_ref):
    n = pl.program_id(1)
    rows = AT_GROUP * BLOCK
    ri = lax.broadcasted_iota(jnp.int32, (rows, 2 * BLOCK), 0)
    sj = lax.broadcasted_iota(jnp.int32, (rows, 2 * BLOCK), 1)
    dist = ri % BLOCK + BLOCK - sj
    valid = (dist >= 0) & (dist <= WINDOW) & ((sj >= BLOCK) | (n > 0))
    distf = dist.astype(F32)
    gi = lax.broadcasted_iota(jnp.int32, (rows, 1), 0) // BLOCK
    for kv in range(AT_KV_HEADS):
        ks = slice(kv * AT_HEAD_DIM, (kv + 1) * AT_HEAD_DIM)
        kband = jnp.concatenate([kp_ref[:, ks], kc_ref[:, ks]], axis=0).astype(BF16)
        vband = jnp.concatenate([vp_ref[:, ks], vc_ref[:, ks]], axis=0).astype(BF16)
        kmeta = km_ref[:, ks].astype(BF16)
        vmeta = vm_ref[:, ks].astype(BF16)
        heads = [kv * AT_GROUP + g for g in range(AT_GROUP)]
        slope = jnp.exp2(-(gi + (kv * AT_GROUP + 1)).astype(F32) * (8.0 / AT_HEADS))
        sink = jnp.zeros((rows, 1), F32)
        for g, h in enumerate(heads):
            sink = jnp.where(gi == g, sink_ref[h], sink)
        qs = jnp.concatenate([q_ref[:, h * AT_HEAD_DIM:(h + 1) * AT_HEAD_DIM] for h in heads],
                             axis=0).astype(BF16)
        lband = lax.dot_general(qs, kband, NT_DIMS, preferred_element_type=F32) - slope * distf
        lband = jnp.where(valid, lband, -jnp.inf)
        lmeta = lax.dot_general(qs, kmeta, NT_DIMS, preferred_element_type=F32)
        m = jnp.maximum(jnp.maximum(jnp.max(lband, axis=-1, keepdims=True),
                                    jnp.max(lmeta, axis=-1, keepdims=True)), sink)
        eb = jnp.exp(lband - m)
        em = jnp.exp(lmeta - m)
        den = (jnp.sum(eb, axis=-1, keepdims=True) + jnp.sum(em, axis=-1, keepdims=True)
               + jnp.exp(sink - m))
        o = (jnp.dot(eb.astype(BF16), vband, preferred_element_type=F32)
             + jnp.dot(em.astype(BF16), vmeta, preferred_element_type=F32)) / den
        for g, h in enumerate(heads):
            o_ref[:, h * AT_HEAD_DIM:(h + 1) * AT_HEAD_DIM] = o[g * BLOCK:(g + 1) * BLOCK]


def _swa_prompt(sinks, aq, ak, av, km, vm, batch):
    n = aq.shape[0]
    nb = n // batch // BLOCK
    cur = lambda w: pl.BlockSpec((BLOCK, w), lambda b, i: (b * nb + i, 0))
    prev = lambda w: pl.BlockSpec((BLOCK, w), lambda b, i: (b * nb + jnp.maximum(i - 1, 0), 0))
    meta = pl.BlockSpec((N_META, AT_KV_WIDTH), lambda b, i: (0, 0))
    return pl.pallas_call(
        _swa_prompt_kernel,
        grid=(batch, nb),
        in_specs=[pl.BlockSpec(memory_space=pltpu.SMEM), cur(AT_WIDTH), cur(AT_KV_WIDTH), prev(AT_KV_WIDTH),
                  cur(AT_KV_WIDTH), prev(AT_KV_WIDTH), meta, meta],
        out_specs=cur(AT_WIDTH),
        out_shape=jax.ShapeDtypeStruct((n, AT_WIDTH), F32),
        compiler_params=_cparams("parallel", "parallel"),
        name="swa_prompt",
    )(sinks, aq, ak, ak, av, av, km, vm)


def _swa_sample_kernel(sink_ref, q_ref, kn_ref, vn_ref, kw_ref, vw_ref, km_ref, vm_ref, o_ref, *, bg, ds):
    rows = AT_GROUP * ds
    nk = WINDOW + ds
    ri = lax.broadcasted_iota(jnp.int32, (rows, nk), 0)
    sj = lax.broadcasted_iota(jnp.int32, (rows, nk), 1)
    qpos = PAST_LEN + ri % ds
    kpos = PAST_LEN - WINDOW + sj
    dist = qpos - kpos
    valid = (dist >= 0) & (dist <= WINDOW) & (kpos >= N_META)
    distf = dist.astype(F32)
    gi = lax.broadcasted_iota(jnp.int32, (rows, 1), 0) // ds
    for kv in range(AT_KV_HEADS):
        ks = slice(kv * AT_HEAD_DIM, (kv + 1) * AT_HEAD_DIM)
        slope = jnp.exp2(-(gi + (kv * AT_GROUP + 1)).astype(F32) * (8.0 / AT_HEADS))
        sink = jnp.zeros((rows, 1), F32)
        for g in range(AT_GROUP):
            sink = jnp.where(gi == g, sink_ref[kv * AT_GROUP + g], sink)
        for b in range(bg):
            kall = jnp.concatenate([kw_ref[b, :, ks], kn_ref[b, :, ks]], axis=0).astype(BF16)
            vall = jnp.concatenate([vw_ref[b, :, ks], vn_ref[b, :, ks]], axis=0).astype(BF16)
            kmeta = km_ref[b, :, ks].astype(BF16)
            vmeta = vm_ref[b, :, ks].astype(BF16)
            qs = jnp.concatenate(
                [q_ref[b, :, (kv * AT_GROUP + g) * AT_HEAD_DIM:(kv * AT_GROUP + g + 1) * AT_HEAD_DIM]
                 for g in range(AT_GROUP)], axis=0).astype(BF16)
            lw = lax.dot_general(qs, kall, NT_DIMS, preferred_element_type=F32) - slope * distf
            lw = jnp.where(valid, lw, -jnp.inf)
            lm = lax.dot_general(qs, kmeta, NT_DIMS, preferred_element_type=F32)
            m = jnp.maximum(jnp.maximum(jnp.max(lw, axis=-1, keepdims=True),
                                        jnp.max(lm, axis=-1, keepdims=True)), sink)
            ew = jnp.exp(lw - m)
            em = jnp.exp(lm - m)
            den = (jnp.sum(ew, axis=-1, keepdims=True) + jnp.sum(em, axis=-1, keepdims=True)
                   + jnp.exp(sink - m))
            o = (jnp.dot(ew.astype(BF16), vall, preferred_element_type=F32)
                 + jnp.dot(em.astype(BF16), vmeta, preferred_element_type=F32)) / den
            for g in range(AT_GROUP):
                h = kv * AT_GROUP + g
                o_ref[b, :, h * AT_HEAD_DIM:(h + 1) * AT_HEAD_DIM] = o[g * ds:(g + 1) * ds]


def _swa_sample(sinks, aq, ak, av, win_k, win_v, meta_k, meta_v, bg):
    b, ds = aq.shape[:2]
    spec = lambda a: pl.BlockSpec((bg,) + a.shape[1:], lambda i: (i, 0, 0))
    args = (aq, ak, av, win_k, win_v, meta_k, meta_v)
    return pl.pallas_call(
        functools.partial(_swa_sample_kernel, bg=bg, ds=ds),
        grid=(b // bg,),
        in_specs=[pl.BlockSpec(memory_space=pltpu.SMEM)] + [spec(a) for a in args],
        out_specs=spec(aq),
        out_shape=jax.ShapeDtypeStruct(aq.shape, F32),
        compiler_params=_cparams("parallel"),
        name="swa_sample",
    )(sinks, *args)


def _merge_kernel(x_ref, oa_ref, ob_ref, sga_ref, sgb_ref, wpa_ref, wpb_ref, wo_ref, gain_ref, wq_ref, sk_ref,
                  h2_ref, xn_ref, sc_ref):
    pa = jnp.dot(oa_ref[...].astype(BF16), wpa_ref[...], preferred_element_type=F32)
    pb = jnp.dot(ob_ref[...].astype(BF16), wpb_ref[...], preferred_element_type=F32)
    m = sga_ref[...] * pa + sgb_ref[...] * pb
    h2 = x_ref[...] + jnp.dot(m.astype(BF16), wo_ref[...], preferred_element_type=F32)
    h2_ref[...] = h2
    xn = h2 * lax.rsqrt(jnp.mean(h2 * h2, axis=-1, keepdims=True) + EPS) * gain_ref[...]
    xn_ref[...] = xn
    qp = jnp.dot(xn.astype(BF16), wq_ref[...], preferred_element_type=F32).astype(BF16)
    for hc in range(2 * PEER_HEADS):
        sc_ref[hc] = lax.dot_general(sk_ref[hc], qp[:, hc * PEER_HALF:(hc + 1) * PEER_HALF], NT_DIMS,
                                     preferred_element_type=F32)


def _merge(x, oa, ob, sga, sgb, wpa, wpb, wo, gain, wq, sk, tm):
    n = x.shape[0]
    row = lambda w: pl.BlockSpec((tm, w), lambda i: (i, 0))
    full = lambda a: pl.BlockSpec(a.shape, lambda i: (0,) * a.ndim)
    return pl.pallas_call(
        _merge_kernel,
        grid=(n // tm,),
        in_specs=[row(D_MODEL), row(HG_WIDTH), row(AT_WIDTH), row(D_MODEL), row(D_MODEL),
                  full(wpa), full(wpb), full(wo), full(gain), full(wq), full(sk)],
        out_specs=[row(D_MODEL), row(D_MODEL),
                   pl.BlockSpec((2 * PEER_HEADS, PEER_NKEYS, tm), lambda i: (0, 0, i))],
        out_shape=[jax.ShapeDtypeStruct((n, D_MODEL), F32), jax.ShapeDtypeStruct((n, D_MODEL), F32),
                   jax.ShapeDtypeStruct((2 * PEER_HEADS, PEER_NKEYS, n), F32)],
        compiler_params=_cparams("parallel"),
        name="merge",
    )(x, oa, ob, sga, sgb, wpa, wpb, wo, gain, wq, sk)


def _top16(arrays):
    arrays = list(arrays)
    t = arrays[0].shape[1]
    io16 = lax.broadcasted_iota(jnp.int32, (PEER_TOPK, t), 0)
    ios = [lax.broadcasted_iota(jnp.int32, s.shape, 0).astype(F32) for s in arrays]
    vals = [jnp.zeros((PEER_TOPK, t), F32) for _ in arrays]
    idxs = [jnp.zeros((PEER_TOPK, t), F32) for _ in arrays]
    for j in range(PEER_TOPK):
        for k, (s, io) in enumerate(zip(arrays, ios)):
            m = jnp.max(s, axis=0, keepdims=True)
            idx = jnp.min(jnp.where(s == m, io, float(s.shape[0])), axis=0, keepdims=True)
            vals[k] = jnp.where(io16 == j, m, vals[k])
            idxs[k] = jnp.where(io16 == j, idx, idxs[k])
            arrays[k] = jnp.where(io == idx, -jnp.inf, s)
    return vals, [i.astype(jnp.int32) for i in idxs]


_CAND_A0, _CAND_MID, _CAND_ROWS = PEER_TOPK, PEER_TOPK + 7 * SUBLANES, PEER_TOPK + 8 * SUBLANES


def _candidates(v1, v2):
    t = v1.shape[1]
    io8 = lax.broadcasted_iota(jnp.int32, (SUBLANES, t), 0)
    pieces = [v1[0:1] + v2]
    for a in range(1, SUBLANES):
        keep = PEER_TOPK // (a + 1)
        p = v1[a:a + 1] + v2[0:SUBLANES]
        pieces.append(p if keep >= SUBLANES else jnp.where(io8 < keep, p, -jnp.inf))
    pieces.append(v1[SUBLANES:] + v2[0:1])
    return jnp.concatenate(pieces, axis=0)


def _candidate_ab(pos):
    mid = pos - _CAND_A0
    a = jnp.where(pos < _CAND_A0, 0, jnp.where(pos < _CAND_MID, (mid >> 3) + 1, pos - (_CAND_MID - SUBLANES)))
    b = jnp.where(pos < _CAND_A0, pos, jnp.where(pos < _CAND_MID, mid & (SUBLANES - 1), 0))
    return a, b


def _topk_kernel(s_ref, eidx_ref, gate_ref):
    for h0 in range(0, PEER_HEADS, 2):
        heads = (h0, h0 + 1)
        sub = [_top16([s_ref[2 * h], s_ref[2 * h + 1]]) for h in heads]
        ctops, cposs = _top16([_candidates(vals[0], vals[1]) for vals, _ in sub])
        for h, (_, (i1, i2)), ctop, cpos in zip(heads, sub, ctops, cposs):
            a, b = _candidate_ab(cpos)
            e1 = jnp.zeros_like(a)
            e2 = jnp.zeros_like(b)
            for r in range(PEER_TOPK):
                e1 = jnp.where(a == r, i1[r:r + 1], e1)
                e2 = jnp.where(b == r, i2[r:r + 1], e2)
            ex = jnp.exp(ctop - ctop[0:1])
            sl = slice(h * PEER_TOPK, (h + 1) * PEER_TOPK)
            gate_ref[sl, :] = ex / jnp.sum(ex, axis=0, keepdims=True)
            eidx_ref[sl, :] = e1 * PEER_NKEYS + e2


def _topk(scores_t, tk):
    n = scores_t.shape[2]
    return pl.pallas_call(
        _topk_kernel,
        grid=(n // tk,),
        in_specs=[pl.BlockSpec((2 * PEER_HEADS, PEER_NKEYS, tk), lambda i: (0, 0, i))],
        out_specs=[pl.BlockSpec((PEER_PICKS, tk), lambda i: (0, i))] * 2,
        out_shape=[jax.ShapeDtypeStruct((PEER_PICKS, n), jnp.int32),
                   jax.ShapeDtypeStruct((PEER_PICKS, n), F32)],
        compiler_params=_cparams("parallel"),
        name="topk",
    )(scores_t)


ROW_PARTS = SUBLANES // 2


def _pack_rows_kernel(x_ref, o_ref):
    x = x_ref[...]
    for s in range(ROW_PARTS):
        even = x[:, (2 * s) * LANES:(2 * s + 1) * LANES]
        odd = x[:, (2 * s + 1) * LANES:(2 * s + 2) * LANES]
        words = pltpu.pack_elementwise([even, odd], packed_dtype=BF16)
        o_ref[:, s * LANES:(s + 1) * LANES] = pltpu.bitcast(words, jnp.int32)


def _table_rows(tab):
    e = tab.shape[0]
    tm = _tile(e, 1024)
    words = pl.pallas_call(
        _pack_rows_kernel,
        grid=(e // tm,),
        in_specs=[pl.BlockSpec((tm, D_MODEL), lambda i: (i, 0))],
        out_specs=pl.BlockSpec((tm, ROW_PARTS * LANES), lambda i: (i, 0)),
        out_shape=jax.ShapeDtypeStruct((e, ROW_PARTS * LANES), jnp.int32),
        compiler_params=_cparams("parallel"),
        name="pack_rows",
    )(tab)
    return words.reshape(e * ROW_PARTS, LANES)


def _load_row(tab_ref, offset):
    words = tab_ref[pl.ds(pl.multiple_of(offset, ROW_PARTS), ROW_PARTS), :]
    return pltpu.bitcast(words, BF16).astype(F32)


_BITREV = tuple(int(format(i, "03b")[::-1], 2) for i in range(SUBLANES))


def _fold_sublanes(a, b, half):
    s = lax.broadcasted_iota(jnp.int32, (SUBLANES, LANES), 0)
    low = (s & half) == 0
    if 2 * half < SUBLANES:
        b = pltpu.roll(b, half, axis=0)
    return jnp.where(low, a, b) + pltpu.roll(jnp.where(low, b, a), SUBLANES - half, axis=0)


def _split_indices(eidx_t):
    n = eidx_t.shape[1]
    return eidx_t.reshape(PEER_OCTETS, SUBLANES, n).transpose(1, 2, 0).reshape(SUBLANES, n * PEER_OCTETS)


def _for_each_index_slot(idx_hbm, sm_refs, sem, tb, process):
    step = pl.program_id(0)
    words = tb * PEER_OCTETS

    def copies(block, slot):
        return [pltpu.make_async_copy(idx_hbm.at[i, pl.ds(block * words, words)], sm_refs[slot][i], sem.at[slot, i])
                for i in range(SUBLANES)]

    def start(block, slot):
        for c in copies(block, slot):
            c.start()

    @pl.when(step == 0)
    def _():
        start(0, 0)

    for slot in range(PEER_SLOTS):
        block = step * PEER_SLOTS + slot
        for c in copies(block, slot):
            c.wait()
        if slot + 1 < PEER_SLOTS:
            start(block + 1, slot + 1)
        else:
            @pl.when(step + 1 < pl.num_programs(0))
            def _():
                start(block + 1, 0)
        process(slot)


def _index_scratch(tb):
    return ([pltpu.SMEM((tb * PEER_OCTETS,), jnp.int32) for _ in range(PEER_SLOTS * SUBLANES)]
            + [pltpu.SemaphoreType.DMA((PEER_SLOTS, SUBLANES))])


def _index_refs(refs):
    return [refs[s * SUBLANES:(s + 1) * SUBLANES] for s in range(PEER_SLOTS)]


def _peer_a_kernel(idx_hbm, x_ref, gate_ref, tab_ref, w_ref, *scratch, tb):
    sm_refs, sem, q_ref, h_ref = _index_refs(scratch), scratch[-3], scratch[-2], scratch[-1]
    lane = lax.broadcasted_iota(jnp.int32, (SUBLANES, tb), 1)

    def place(sums, t):
        h = h_ref[...]
        for octet, col in enumerate(sums):
            rows = slice(octet * SUBLANES, (octet + 1) * SUBLANES)
            h_ref[rows, :] = jnp.where(lane == t, col, h[rows, :])

    def process(slot):
        t0 = slot * tb
        sm = sm_refs[slot]
        h_ref[...] = jnp.zeros_like(h_ref)
        q_ref[...] = jnp.zeros_like(q_ref)

        def lane_sums():
            q = q_ref[...]
            return [jnp.sum(q[o * SUBLANES:(o + 1) * SUBLANES, :], axis=1, keepdims=True) for o in range(PEER_OCTETS)]

        def tok(t, carry):
            sums = lane_sums()
            xv = x_ref[t0 + t]
            for octet in range(PEER_OCTETS):
                base = t * PEER_OCTETS + octet
                tiles = [_load_row(tab_ref, sm[_BITREV[i]][base]) * xv for i in range(SUBLANES)]
                half = SUBLANES // 2
                while len(tiles) > 1:
                    tiles = [_fold_sublanes(tiles[i], tiles[i + 1], half) for i in range(0, len(tiles), 2)]
                    half //= 2
                q_ref[octet * SUBLANES:(octet + 1) * SUBLANES, :] = tiles[0]
            place(sums, t - 1)
            return carry

        lax.fori_loop(0, tb, tok, 0)
        place(lane_sums(), tb - 1)
        h = h_ref[...]
        w_ref[:, t0:t0 + tb] = gate_ref[:, t0:t0 + tb] * (0.5 * h * (1.0 + lax.erf(h * math.sqrt(0.5))))

    _for_each_index_slot(idx_hbm, sm_refs, sem, tb, process)


def _peer_a(eidx_split, xn3, gate_t, tab, tb):
    n = xn3.shape[0]
    span = PEER_SLOTS * tb
    return pl.pallas_call(
        functools.partial(_peer_a_kernel, tb=tb),
        grid=(n // span,),
        in_specs=[pl.BlockSpec(memory_space=pl.ANY),
                  pl.BlockSpec((span, SUBLANES, LANES), lambda i: (i, 0, 0)),
                  pl.BlockSpec((PEER_PICKS, span), lambda i: (0, i)),
                  pl.BlockSpec(tab.shape, lambda i: (0, 0), pipeline_mode=pl.Buffered(1))],
        out_specs=pl.BlockSpec((PEER_PICKS, span), lambda i: (0, i)),
        out_shape=jax.ShapeDtypeStruct((PEER_PICKS, n), F32),
        scratch_shapes=_index_scratch(tb) + [pltpu.VMEM((PEER_PICKS, LANES), F32), pltpu.VMEM((PEER_PICKS, tb), F32)],
        compiler_params=_cparams("arbitrary"),
        name="peer_a",
    )(eidx_split, xn3, gate_t, tab)


def _peer_b_kernel(idx_hbm, wt_ref, h2_ref, gain_ref, tab_ref, y_ref, *scratch, tb):
    sm_refs, sem, wb_refs = _index_refs(scratch), scratch[-3], scratch[-2:]
    nacc = 2

    def process(slot):
        t0 = slot * tb
        sm = sm_refs[slot]
        wt = wt_ref[:, t0:t0 + tb]

        def spread(wb_ref, t):
            wb_ref[...] = jnp.take_along_axis(wt, jnp.full((PEER_PICKS, LANES), t, jnp.int32), axis=1)

        spread(wb_refs[0], 0)

        def pair(tp, carry):
            for par in range(2):
                t = 2 * tp + par
                wb_ref = wb_refs[par]
                spread(wb_refs[1 - par], jnp.minimum(t + 1, tb - 1))
                acc = [jnp.zeros((SUBLANES, LANES), F32) for _ in range(nacc)]
                for octet in range(PEER_OCTETS):
                    base = t * PEER_OCTETS + octet
                    wg = wb_ref[octet * SUBLANES:(octet + 1) * SUBLANES, :]
                    for i in range(SUBLANES):
                        acc[i % nacc] = acc[i % nacc] + wg[i:i + 1, :] * _load_row(tab_ref, sm[i][base])
                total = acc[0]
                for a in acc[1:]:
                    total = total + a
                y_ref[t0 + t] = h2_ref[t0 + t] + total
            return carry

        lax.fori_loop(0, tb // 2, pair, 0)

    _for_each_index_slot(idx_hbm, sm_refs, sem, tb, process)

    y = y_ref[...]
    ms = jnp.sum(jnp.sum(y * y, axis=2, keepdims=True), axis=1, keepdims=True) * (1.0 / D_MODEL)
    y_ref[...] = y * lax.rsqrt(ms + EPS) * gain_ref[...][None]


def _peer_b(eidx_split, w_t, h23, gain3, tab, tb):
    n = h23.shape[0]
    span = PEER_SLOTS * tb
    tok3 = pl.BlockSpec((span, SUBLANES, LANES), lambda i: (i, 0, 0))
    return pl.pallas_call(
        functools.partial(_peer_b_kernel, tb=tb),
        grid=(n // span,),
        in_specs=[pl.BlockSpec(memory_space=pl.ANY),
                  pl.BlockSpec((PEER_PICKS, span), lambda i: (0, i)),
                  tok3, pl.BlockSpec(gain3.shape, lambda i: (0, 0)),
                  pl.BlockSpec(tab.shape, lambda i: (0, 0), pipeline_mode=pl.Buffered(1))],
        out_specs=tok3,
        out_shape=jax.ShapeDtypeStruct(h23.shape, F32),
        scratch_shapes=_index_scratch(tb) + [pltpu.VMEM((PEER_PICKS, LANES), F32) for _ in range(2)],
        compiler_params=_cparams("arbitrary"),
        name="peer_b",
    )(eidx_split, w_t, h23, gain3, tab)


def _mix_and_ffn(x, oa, ob, sga, sgb, wts):
    n = x.shape[0]
    tm, tk, tb = _tile(n, 256), _tile(n, LANES), LANES
    assert n % (PEER_SLOTS * tb) == 0, "PEER kernels take whole groups of token blocks"
    h2, xn, scores_t = _merge(x, oa, ob, sga, sgb, wts["wpa"], wts["wpb"], wts["wo"], wts["gain_ffn"],
                              wts["wq"], wts["sk"], tm)
    eidx_t, gate_t = _topk(scores_t, tk)
    eidx_split = _split_indices(eidx_t * ROW_PARTS)
    w_t = _peer_a(eidx_split, xn.reshape(n, SUBLANES, LANES), gate_t, wts["u"], tb)
    y3 = _peer_b(eidx_split, w_t, h2.reshape(n, SUBLANES, LANES), wts["gain_final"], wts["v"], tb)
    return y3.reshape(n, D_MODEL)


def kernel(x_prompt, x_sample, cache_win_k, cache_win_v, cache_meta_k, cache_meta_v, state_hgrn, meta_tokens,
           norm_mix, w_in, hg_lb, hg_norm, attn_sinks, w_pa, w_pb, w_o, norm_ffn, peer_wq, peer_subkeys,
           peer_u, peer_v, norm_final):
    depth = w_in.shape[0]
    assert depth == 1, "single-layer step only"
    b, seq, _ = x_prompt.shape
    db, ds, _ = x_sample.shape
    l = 0
    lb = jnp.cumsum(jax.nn.softmax(hg_lb.astype(F32), axis=0), axis=0)[l][None, :]
    gain_mix = norm_mix[l][None, :]
    w_in_b = w_in[l].astype(BF16)
    hg_gain = hg_norm[l][None, :]
    sinks = attn_sinks[l].astype(F32)
    wts = dict(
        wpa=w_pa[l].astype(BF16), wpb=w_pb[l].astype(BF16), wo=w_o[l].astype(BF16),
        gain_ffn=norm_ffn[l][None, :], wq=peer_wq[l].astype(BF16),
        sk=peer_subkeys[l].reshape(2 * PEER_HEADS, PEER_NKEYS, PEER_HALF).astype(BF16),
        u=_table_rows(peer_u[l]), v=_table_rows(peer_v[l]),
        gain_final=norm_final.reshape(SUBLANES, LANES),
    )

    mq, mk, mlf, mv, msg, _, km, vm, _, _ = _proj(meta_tokens, gain_mix, lb, w_in_b, N_META)
    zero_state = jnp.zeros((1, HG_HEADS, HG_DK, HG_DK), F32)
    _, s_meta = _hgrn(mq, mk, mlf, mv, msg, hg_gain, zero_state, 1, N_META, N_META)

    xp = x_prompt.reshape(b * seq, D_MODEL)
    q, k, lf, v, sg, aq, ak, av, sga, sgb = _proj(xp, gain_mix, lb, w_in_b, _tile(b * seq, 256))
    oa, s_fin = _hgrn(q, k, lf, v, sg, hg_gain, s_meta, b, HG_CHUNK, _tile(seq, 512))
    ob = _swa_prompt(sinks, aq, ak, av, km, vm, b)
    y_prompt = _mix_and_ffn(xp, oa, ob, sga, sgb, wts).reshape(b, seq, D_MODEL)
    akr = ak.reshape(b, seq, AT_KV_HEADS, AT_HEAD_DIM)
    avr = av.reshape(b, seq, AT_KV_HEADS, AT_HEAD_DIM)
    kmr = jnp.broadcast_to(km.reshape(1, N_META, AT_KV_HEADS, AT_HEAD_DIM), (b, N_META, AT_KV_HEADS, AT_HEAD_DIM))
    vmr = jnp.broadcast_to(vm.reshape(1, N_META, AT_KV_HEADS, AT_HEAD_DIM), (b, N_META, AT_KV_HEADS, AT_HEAD_DIM))

    xs = x_sample.reshape(db * ds, D_MODEL)
    q, k, lf, v, sg, aq, aks, avs, sga, sgb = _proj(xs, gain_mix, lb, w_in_b, _tile(db * ds, 256))
    cs = SUBLANES
    pad = lambda a: jnp.pad(a.reshape(db, ds, HG_WIDTH), ((0, 0), (0, cs - ds), (0, 0))).reshape(db * cs, HG_WIDTH)
    oa, s_new = _hgrn(pad(q), pad(k), pad(lf), pad(v), pad(sg), hg_gain, state_hgrn[l].astype(F32), db, cs, cs)
    oa = oa.reshape(db, cs, HG_WIDTH)[:, :ds].reshape(db * ds, HG_WIDTH)
    ob = _swa_sample(sinks, aq.reshape(db, ds, AT_WIDTH), aks.reshape(db, ds, AT_KV_WIDTH),
                     avs.reshape(db, ds, AT_KV_WIDTH),
                     cache_win_k[l].reshape(db, WINDOW, AT_KV_WIDTH), cache_win_v[l].reshape(db, WINDOW, AT_KV_WIDTH),
                     cache_meta_k[l].reshape(db, N_META, AT_KV_WIDTH), cache_meta_v[l].reshape(db, N_META, AT_KV_WIDTH),
                     8).reshape(db * ds, AT_WIDTH)
    y_sample = _mix_and_ffn(xs, oa, ob, sga, sgb, wts).reshape(db, ds, D_MODEL)

    return (y_prompt, y_sample,
            akr[:, -WINDOW:][None], avr[:, -WINDOW:][None], kmr[None], vmr[None], s_fin[None],
            aks.reshape(1, db, ds, AT_KV_HEADS, AT_HEAD_DIM), avs.reshape(1, db, ds, AT_KV_HEADS, AT_HEAD_DIM),
            s_new[None])
```

```python
import functools
import math

import numpy as np
import jax
import jax.numpy as jnp
from jax import lax
from jax.experimental import pallas as pl
from jax.experimental.pallas import tpu as pltpu

F32 = jnp.float32
BF16 = jnp.bfloat16
EPS = 1e-6
D_MODEL = 1024
N_META = 16
HG_HEADS = 4
HG_DK = 128
HG_WIDTH = HG_HEADS * HG_DK
HG_CHUNK = 64
HG_CHUNK_UNROLL = 8
AT_HEADS = 8
AT_KV_HEADS = 2
AT_GROUP = AT_HEADS // AT_KV_HEADS
AT_HEAD_DIM = 64
AT_WIDTH = AT_HEADS * AT_HEAD_DIM
AT_KV_WIDTH = AT_KV_HEADS * AT_HEAD_DIM
AT_SCALE = AT_HEAD_DIM ** -0.5
WINDOW = 128
BLOCK = 128
PAST_LEN = 16384
PEER_HEADS = 8
PEER_NKEYS = 128
PEER_HALF = 128
PEER_TOPK = 16
PEER_PICKS = PEER_HEADS * PEER_TOPK
SUBLANES = 8
LANES = 128
PEER_OCTETS = PEER_PICKS // SUBLANES
PEER_SLOTS = 2
VMEM_LIMIT = 56 * 1024 * 1024

NT_DIMS = (((1,), (1,)), ((), ()))
TN_DIMS = (((0,), (0,)), ((), ()))


def _cparams(*sem):
    return pltpu.CompilerParams(dimension_semantics=sem, vmem_limit_bytes=VMEM_LIMIT)


def _tile(n, pref):
    return pref if n % pref == 0 else n


def _sigmoid(x):
    return 1.0 / (1.0 + jnp.exp(-x))


def _proj_kernel(x_ref, gain_ref, lb_ref, w_ref, q_ref, k_ref, lf_ref, v_ref, sg_ref,
                 aq_ref, ak_ref, av_ref, sga_ref, sgb_ref):
    x = x_ref[...]
    xn = x * lax.rsqrt(jnp.mean(x * x, axis=-1, keepdims=True) + EPS) * gain_ref[...]
    xb = xn.astype(BF16)

    def proj(a, b):
        return jnp.dot(xb, w_ref[:, a:b], preferred_element_type=F32)

    o = 0
    hq = proj(o, o + HG_WIDTH); o += HG_WIDTH
    q_ref[...] = hq * _sigmoid(hq)
    lb = lb_ref[...]
    f = lb + (1.0 - lb) * _sigmoid(proj(o, o + HG_WIDTH)); o += HG_WIDTH
    k_ref[...] = 1.0 - f
    lf_ref[...] = jnp.log(f)
    v_ref[...] = proj(o, o + HG_WIDTH); o += HG_WIDTH
    hg = proj(o, o + HG_WIDTH); o += HG_WIDTH
    sg_ref[...] = hg * _sigmoid(hg)
    aq_ref[...] = proj(o, o + AT_WIDTH) * AT_SCALE; o += AT_WIDTH
    ak_ref[...] = proj(o, o + AT_KV_WIDTH); o += AT_KV_WIDTH
    av_ref[...] = proj(o, o + AT_KV_WIDTH); o += AT_KV_WIDTH
    sga_ref[...] = _sigmoid(proj(o, o + D_MODEL)); o += D_MODEL
    sgb_ref[...] = _sigmoid(proj(o, o + D_MODEL))


def _proj(x, gain, lb, w_bf16, tm):
    n = x.shape[0]
    widths = (HG_WIDTH,) * 5 + (AT_WIDTH, AT_KV_WIDTH, AT_KV_WIDTH, D_MODEL, D_MODEL)
    row = lambda w: pl.BlockSpec((tm, w), lambda i: (i, 0))
    full = lambda a: pl.BlockSpec(a.shape, lambda i: (0,) * a.ndim)
    return pl.pallas_call(
        _proj_kernel,
        grid=(n // tm,),
        in_specs=[row(D_MODEL), full(gain), full(lb), full(w_bf16)],
        out_specs=[row(w) for w in widths],
        out_shape=[jax.ShapeDtypeStruct((n, w), F32) for w in widths],
        compiler_params=_cparams("parallel"),
        name="proj",
    )(x, gain, lb, w_bf16)


def _hgrn_consts(c):
    levels = int(round(math.log2(c)))
    t = np.arange(c)
    le = (t[None, :] <= t[:, None]).astype(np.float32)
    mats = [le]
    masks = []
    for l in range(levels):
        bs, half = 2 << l, 1 << l
        base = (t // bs) * bs
        bnd = base + half - 1
        mats.append(le - (t[None, :] <= bnd[:, None]).astype(np.float32))
        right = (t % bs) >= half
        masks.append((base[:, None] == base[None, :]) & right[:, None] & (~right)[None, :])
    masks.append(np.eye(c, dtype=bool))
    a = np.concatenate(mats, 0)
    a2 = np.concatenate([a, a], 1)
    return jnp.asarray(a2, BF16), jnp.asarray(np.stack(masks).astype(np.float32))


def _hgrn_kernel(a_ref, m_ref, q_ref, k_ref, lf_ref, v_ref, sg_ref, gain_ref, s0_ref, o_ref, sfin_ref, *st_ref,
                 c, nchunk):
    levels = int(round(math.log2(c)))
    i = pl.program_id(1)

    @pl.when(i == 0)
    def _():
        for h in range(HG_HEADS):
            st_ref[h][...] = s0_ref[0, h].T

    gain = gain_ref[...]

    def chunk(ci, carry):
        r0 = pl.multiple_of(ci * c, c)
        rows = pl.ds(r0, c)
        heads = range(HG_HEADS)
        cols = [slice(h * HG_DK, (h + 1) * HG_DK) for h in heads]
        lf = lf_ref[rows, :]
        hi = lf.astype(BF16)
        lo = (lf - hi.astype(F32)).astype(BF16)
        r_all = jnp.dot(a_ref[...], jnp.concatenate([hi, lo], axis=0), preferred_element_type=F32)
        q = [q_ref[rows, cs] for cs in cols]
        k = [k_ref[rows, cs] for cs in cols]
        vb = [v_ref[rows, cs].astype(BF16) for cs in cols]
        st = [st_ref[h][...] for h in heads]
        r = [r_all[:, cs] for cs in cols]
        cum = [r[h][0:c] for h in heads]
        o = [lax.dot_general((q[h] * jnp.exp(cum[h])).astype(BF16), st[h].astype(BF16), NT_DIMS,
                             preferred_element_type=F32) for h in heads]
        att = [m_ref[levels] * lax.dot_general(q[h].astype(BF16), k[h].astype(BF16), NT_DIMS,
                                               preferred_element_type=F32) for h in heads]
        for l in range(levels):
            for h in heads:
                e = r[h][(l + 1) * c:(l + 2) * c]
                aq = (q[h] * jnp.exp(jnp.minimum(e, 0.0))).astype(BF16)
                ak = (k[h] * jnp.exp(jnp.minimum(-e, 0.0))).astype(BF16)
                att[h] = att[h] + m_ref[l] * lax.dot_general(aq, ak, NT_DIMS, preferred_element_type=F32)
        o = [o[h] + jnp.dot(att[h].astype(BF16), vb[h], preferred_element_type=F32) for h in heads]
        for h in heads:
            last = cum[h][c - 1:c]
            kdec = (k[h] * jnp.exp(last - cum[h])).astype(BF16)
            st_ref[h][...] = st[h] * jnp.exp(last) + lax.dot_general(vb[h], kdec, TN_DIMS,
                                                                    preferred_element_type=F32)
        for h in heads:
            on = o[h] * lax.rsqrt(jnp.mean(o[h] * o[h], axis=-1, keepdims=True) + EPS) * gain
            o_ref[rows, cols[h]] = on * sg_ref[rows, cols[h]]
        return carry

    unroll = math.gcd(nchunk, HG_CHUNK_UNROLL)

    def chunks(cj, carry):
        for u in range(unroll):
            chunk(cj * unroll + u, carry)
        return carry

    lax.fori_loop(0, nchunk // unroll, chunks, 0)

    @pl.when(i == pl.num_programs(1) - 1)
    def _():
        for h in range(HG_HEADS):
            sfin_ref[0, h] = st_ref[h][...].T


def _hgrn(q, k, lf, v, sg, gain, s0, batch, c, ct):
    n = q.shape[0]
    t = n // batch
    steps = t // ct
    a2, masks = _hgrn_consts(c)
    row = pl.BlockSpec((ct, HG_WIDTH), lambda b, i: (b * steps + i, 0))
    full = lambda a: pl.BlockSpec(a.shape, lambda b, i: (0,) * a.ndim)
    shared = s0.shape[0] == 1
    sspec = pl.BlockSpec((1, HG_HEADS, HG_DK, HG_DK), (lambda b, i: (0, 0, 0, 0)) if shared else (lambda b, i: (b, 0, 0, 0)))
    return pl.pallas_call(
        functools.partial(_hgrn_kernel, c=c, nchunk=ct // c),
        grid=(batch, steps),
        in_specs=[full(a2), full(masks), row, row, row, row, row, full(gain), sspec],
        out_specs=[row, pl.BlockSpec((1, HG_HEADS, HG_DK, HG_DK), lambda b, i: (b, 0, 0, 0))],
        out_shape=[jax.ShapeDtypeStruct((n, HG_WIDTH), F32),
                   jax.ShapeDtypeStruct((batch, HG_HEADS, HG_DK, HG_DK), F32)],
        scratch_shapes=[pltpu.VMEM((HG_DK, HG_DK), F32) for _ in range(HG_HEADS)],
        compiler_params=_cparams("parallel", "arbitrary"),
        name="hgrn",
    )(a2, masks, q, k, lf, v, sg, gain, s0)


def _swa_prompt_kernel(sink_ref, q_ref, kc_ref, kp_ref, vc_ref, vp_ref, km_ref, vm_ref, o_ref):
    n = pl.program_id(1)
    rows = AT_GROUP * BLOCK
    ri = lax.broadcasted_iota(jnp.int32, (rows, 2 * BLOCK), 0)
    sj = lax.broadcasted_iota(jnp.int32, (rows, 2 * BLOCK), 1)
    dist = ri % BLOCK + BLOCK - sj
    valid = (dist >= 0) & (dist <= WINDOW) & ((sj >= BLOCK) | (n > 0))
    distf = dist.astype(F32)
    gi = lax.broadcasted_iota(jnp.int32, (rows, 1), 0) // BLOCK
    for kv in range(AT_KV_HEADS):
        ks = slice(kv * AT_HEAD_DIM, (kv + 1) * AT_HEAD_DIM)
        kband = jnp.concatenate([kp_ref[:, ks], kc_ref[:, ks]], axis=0).astype(BF16)
        vband = jnp.concatenate([vp_ref[:, ks], vc_ref[:, ks]], axis=0).astype(BF16)
        kmeta = km_ref[:, ks].astype(BF16)
        vmeta = vm_ref[:, ks].astype(BF16)
        heads = [kv * AT_GROUP + g for g in range(AT_GROUP)]
        slope = jnp.exp2(-(gi + (kv * AT_GROUP + 1)).astype(F32) * (8.0 / AT_HEADS))
        sink = jnp.zeros((rows, 1), F32)
        for g, h in enumerate(heads):
            sink = jnp.where(gi == g, sink_ref[h], sink)
        qs = jnp.concatenate([q_ref[:, h * AT_HEAD_DIM:(h + 1) * AT_HEAD_DIM] for h in heads],
                             axis=0).astype(BF16)
        lband = lax.dot_general(qs, kband, NT_DIMS, preferred_element_type=F32) - slope * distf
        lband = jnp.where(valid, lband, -jnp.inf)
        lmeta = lax.dot_general(qs, kmeta, NT_DIMS, preferred_element_type=F32)
        m = jnp.maximum(jnp.maximum(jnp.max(lband, axis=-1, keepdims=True),
                                    jnp.max(lmeta, axis=-1, keepdims=True)), sink)
        eb = jnp.exp(lband - m)
        em = jnp.exp(lmeta - m)
        den = (jnp.sum(eb, axis=-1, keepdims=True) + jnp.sum(em, axis=-1, keepdims=True)
               + jnp.exp(sink - m))
        o = (jnp.dot(eb.astype(BF16), vband, preferred_element_type=F32)
             + jnp.dot(em.astype(BF16), vmeta, preferred_element_type=F32)) / den
        for g, h in enumerate(heads):
            o_ref[:, h * AT_HEAD_DIM:(h + 1) * AT_HEAD_DIM] = o[g * BLOCK:(g + 1) * BLOCK]


def _swa_prompt(sinks, aq, ak, av, km, vm, batch):
    n = aq.shape[0]
    nb = n // batch // BLOCK
    cur = lambda w: pl.BlockSpec((BLOCK, w), lambda b, i: (b * nb + i, 0))
    prev = lambda w: pl.BlockSpec((BLOCK, w), lambda b, i: (b * nb + jnp.maximum(i - 1, 0), 0))
    meta = pl.BlockSpec((N_META, AT_KV_WIDTH), lambda b, i: (0, 0))
    return pl.pallas_call(
        _swa_prompt_kernel,
        grid=(batch, nb),
        in_specs=[pl.BlockSpec(memory_space=pltpu.SMEM), cur(AT_WIDTH), cur(AT_KV_WIDTH), prev(AT_KV_WIDTH),
                  cur(AT_KV_WIDTH), prev(AT_KV_WIDTH), meta, meta],
        out_specs=cur(AT_WIDTH),
        out_shape=jax.ShapeDtypeStruct((n, AT_WIDTH), F32),
        compiler_params=_cparams("parallel", "parallel"),
        name="swa_prompt",
    )(sinks, aq, ak, ak, av, av, km, vm)


def _swa_sample_kernel(sink_ref, q_ref, kn_ref, vn_ref, kw_ref, vw_ref, km_ref, vm_ref, o_ref, *, bg, ds):
    rows = AT_GROUP * ds
    nk = WINDOW + ds
    ri = lax.broadcasted_iota(jnp.int32, (rows, nk), 0)
    sj = lax.broadcasted_iota(jnp.int32, (rows, nk), 1)
    qpos = PAST_LEN + ri % ds
    kpos = PAST_LEN - WINDOW + sj
    dist = qpos - kpos
    valid = (dist >= 0) & (dist <= WINDOW) & (kpos >= N_META)
    distf = dist.astype(F32)
    gi = lax.broadcasted_iota(jnp.int32, (rows, 1), 0) // ds
    for kv in range(AT_KV_HEADS):
        ks = slice(kv * AT_HEAD_DIM, (kv + 1) * AT_HEAD_DIM)
        slope = jnp.exp2(-(gi + (kv * AT_GROUP + 1)).astype(F32) * (8.0 / AT_HEADS))
        sink = jnp.zeros((rows, 1), F32)
        for g in range(AT_GROUP):
            sink = jnp.where(gi == g, sink_ref[kv * AT_GROUP + g], sink)
        for b in range(bg):
            kall = jnp.concatenate([kw_ref[b, :, ks], kn_ref[b, :, ks]], axis=0).astype(BF16)
            vall = jnp.concatenate([vw_ref[b, :, ks], vn_ref[b, :, ks]], axis=0).astype(BF16)
            kmeta = km_ref[b, :, ks].astype(BF16)
            vmeta = vm_ref[b, :, ks].astype(BF16)
            qs = jnp.concatenate(
                [q_ref[b, :, (kv * AT_GROUP + g) * AT_HEAD_DIM:(kv * AT_GROUP + g + 1) * AT_HEAD_DIM]
                 for g in range(AT_GROUP)], axis=0).astype(BF16)
            lw = lax.dot_general(qs, kall, NT_DIMS, preferred_element_type=F32) - slope * distf
            lw = jnp.where(valid, lw, -jnp.inf)
            lm = lax.dot_general(qs, kmeta, NT_DIMS, preferred_element_type=F32)
            m = jnp.maximum(jnp.maximum(jnp.max(lw, axis=-1, keepdims=True),
                                        jnp.max(lm, axis=-1, keepdims=True)), sink)
            ew = jnp.exp(lw - m)
            em = jnp.exp(lm - m)
            den = (jnp.sum(ew, axis=-1, keepdims=True) + jnp.sum(em, axis=-1, keepdims=True)
                   + jnp.exp(sink - m))
            o = (jnp.dot(ew.astype(BF16), vall, preferred_element_type=F32)
                 + jnp.dot(em.astype(BF16), vmeta, preferred_element_type=F32)) / den
            for g in range(AT_GROUP):
                h = kv * AT_GROUP + g
                o_ref[b, :, h * AT_HEAD_DIM:(h + 1) * AT_HEAD_DIM] = o[g * ds:(g + 1) * ds]


def _swa_sample(sinks, aq, ak, av, win_k, win_v, meta_k, meta_v, bg):
    b, ds = aq.shape[:2]
    spec = lambda a: pl.BlockSpec((bg,) + a.shape[1:], lambda i: (i, 0, 0))
    args = (aq, ak, av, win_k, win_v, meta_k, meta_v)
    return pl.pallas_call(
        functools.partial(_swa_sample_kernel, bg=bg, ds=ds),
        grid=(b // bg,),
        in_specs=[pl.BlockSpec(memory_space=pltpu.SMEM)] + [spec(a) for a in args],
        out_specs=spec(aq),
        out_shape=jax.ShapeDtypeStruct(aq.shape, F32),
        compiler_params=_cparams("parallel"),
        name="swa_sample",
    )(sinks, *args)


def _merge_kernel(x_ref, oa_ref, ob_ref, sga_ref, sgb_ref, wpa_ref, wpb_ref, wo_ref, gain_ref, wq_ref, sk_ref,
                  h2_ref, xn_ref, sc_ref):
    pa = jnp.dot(oa_ref[...].astype(BF16), wpa_ref[...], preferred_element_type=F32)
    pb = jnp.dot(ob_ref[...].astype(BF16), wpb_ref[...], preferred_element_type=F32)
    m = sga_ref[...] * pa + sgb_ref[...] * pb
    h2 = x_ref[...] + jnp.dot(m.astype(BF16), wo_ref[...], preferred_element_type=F32)
    h2_ref[...] = h2
    xn = h2 * lax.rsqrt(jnp.mean(h2 * h2, axis=-1, keepdims=True) + EPS) * gain_ref[...]
    xn_ref[...] = xn
    qp = jnp.dot(xn.astype(BF16), wq_ref[...], preferred_element_type=F32).astype(BF16)
    for hc in range(2 * PEER_HEADS):
        sc_ref[hc] = lax.dot_general(sk_ref[hc], qp[:, hc * PEER_HALF:(hc + 1) * PEER_HALF], NT_DIMS,
                                     preferred_element_type=F32)


def _merge(x, oa, ob, sga, sgb, wpa, wpb, wo, gain, wq, sk, tm):
    n = x.shape[0]
    row = lambda w: pl.BlockSpec((tm, w), lambda i: (i, 0))
    full = lambda a: pl.BlockSpec(a.shape, lambda i: (0,) * a.ndim)
    return pl.pallas_call(
        _merge_kernel,
        grid=(n // tm,),
        in_specs=[row(D_MODEL), row(HG_WIDTH), row(AT_WIDTH), row(D_MODEL), row(D_MODEL),
                  full(wpa), full(wpb), full(wo), full(gain), full(wq), full(sk)],
        out_specs=[row(D_MODEL), row(D_MODEL),
                   pl.BlockSpec((2 * PEER_HEADS, PEER_NKEYS, tm), lambda i: (0, 0, i))],
        out_shape=[jax.ShapeDtypeStruct((n, D_MODEL), F32), jax.ShapeDtypeStruct((n, D_MODEL), F32),
                   jax.ShapeDtypeStruct((2 * PEER_HEADS, PEER_NKEYS, n), F32)],
        compiler_params=_cparams("parallel"),
        name="merge",
    )(x, oa, ob, sga, sgb, wpa, wpb, wo, gain, wq, sk)


def _top16(arrays):
    arrays = list(arrays)
    t = arrays[0].shape[1]
    io16 = lax.broadcasted_iota(jnp.int32, (PEER_TOPK, t), 0)
    ios = [lax.broadcasted_iota(jnp.int32, s.shape, 0).astype(F32) for s in arrays]
    vals = [jnp.zeros((PEER_TOPK, t), F32) for _ in arrays]
    idxs = [jnp.zeros((PEER_TOPK, t), F32) for _ in arrays]
    for j in range(PEER_TOPK):
        for k, (s, io) in enumerate(zip(arrays, ios)):
            m = jnp.max(s, axis=0, keepdims=True)
            idx = jnp.min(jnp.where(s == m, io, float(s.shape[0])), axis=0, keepdims=True)
            vals[k] = jnp.where(io16 == j, m, vals[k])
            idxs[k] = jnp.where(io16 == j, idx, idxs[k])
            arrays[k] = jnp.where(io == idx, -jnp.inf, s)
    return vals, [i.astype(jnp.int32) for i in idxs]


_CAND_A0, _CAND_MID, _CAND_ROWS = PEER_TOPK, PEER_TOPK + 7 * SUBLANES, PEER_TOPK + 8 * SUBLANES


def _candidates(v1, v2):
    t = v1.shape[1]
    io8 = lax.broadcasted_iota(jnp.int32, (SUBLANES, t), 0)
    pieces = [v1[0:1] + v2]
    for a in range(1, SUBLANES):
        keep = PEER_TOPK // (a + 1)
        p = v1[a:a + 1] + v2[0:SUBLANES]
        pieces.append(p if keep >= SUBLANES else jnp.where(io8 < keep, p, -jnp.inf))
    pieces.append(v1[SUBLANES:] + v2[0:1])
    return jnp.concatenate(pieces, axis=0)


def _candidate_ab(pos):
    mid = pos - _CAND_A0
    a = jnp.where(pos < _CAND_A0, 0, jnp.where(pos < _CAND_MID, (mid >> 3) + 1, pos - (_CAND_MID - SUBLANES)))
    b = jnp.where(pos < _CAND_A0, pos, jnp.where(pos < _CAND_MID, mid & (SUBLANES - 1), 0))
    return a, b


def _topk_kernel(s_ref, eidx_ref, gate_ref):
    for h0 in range(0, PEER_HEADS, 2):
        heads = (h0, h0 + 1)
        sub = [_top16([s_ref[2 * h], s_ref[2 * h + 1]]) for h in heads]
        ctops, cposs = _top16([_candidates(vals[0], vals[1]) for vals, _ in sub])
        for h, (_, (i1, i2)), ctop, cpos in zip(heads, sub, ctops, cposs):
            a, b = _candidate_ab(cpos)
            e1 = jnp.zeros_like(a)
            e2 = jnp.zeros_like(b)
            for r in range(PEER_TOPK):
                e1 = jnp.where(a == r, i1[r:r + 1], e1)
                e2 = jnp.where(b == r, i2[r:r + 1], e2)
            ex = jnp.exp(ctop - ctop[0:1])
            sl = slice(h * PEER_TOPK, (h + 1) * PEER_TOPK)
            gate_ref[sl, :] = ex / jnp.sum(ex, axis=0, keepdims=True)
            eidx_ref[sl, :] = e1 * PEER_NKEYS + e2


def _topk(scores_t, tk):
    n = scores_t.shape[2]
    return pl.pallas_call(
        _topk_kernel,
        grid=(n // tk,),
        in_specs=[pl.BlockSpec((2 * PEER_HEADS, PEER_NKEYS, tk), lambda i: (0, 0, i))],
        out_specs=[pl.BlockSpec((PEER_PICKS, tk), lambda i: (0, i))] * 2,
        out_shape=[jax.ShapeDtypeStruct((PEER_PICKS, n), jnp.int32),
                   jax.ShapeDtypeStruct((PEER_PICKS, n), F32)],
        compiler_params=_cparams("parallel"),
        name="topk",
    )(scores_t)


ROW_PARTS = SUBLANES // 2


def _pack_rows_kernel(x_ref, o_ref):
    x = x_ref[...]
    for s in range(ROW_PARTS):
        even = x[:, (2 * s) * LANES:(2 * s + 1) * LANES]
        odd = x[:, (2 * s + 1) * LANES:(2 * s + 2) * LANES]
        words = pltpu.pack_elementwise([even, odd], packed_dtype=BF16)
        o_ref[pl.ds(s, x.shape[0], stride=ROW_PARTS), :] = pltpu.bitcast(words, jnp.int32)


def _table_rows(tab):
    e = tab.shape[0]
    tm = _tile(e, 1024)
    return pl.pallas_call(
        _pack_rows_kernel,
        grid=(e // tm,),
        in_specs=[pl.BlockSpec((tm, D_MODEL), lambda i: (i, 0))],
        out_specs=pl.BlockSpec((tm * ROW_PARTS, LANES), lambda i: (i, 0)),
        out_shape=jax.ShapeDtypeStruct((e * ROW_PARTS, LANES), jnp.int32),
        compiler_params=_cparams("parallel"),
        name="pack_rows",
    )(tab)


def _load_row(tab_ref, offset):
    words = tab_ref[pl.ds(pl.multiple_of(offset, ROW_PARTS), ROW_PARTS), :]
    return pltpu.bitcast(words, BF16).astype(F32)


_BITREV = tuple(int(format(i, "03b")[::-1], 2) for i in range(SUBLANES))


def _fold_sublanes(a, b, half):
    s = lax.broadcasted_iota(jnp.int32, (SUBLANES, LANES), 0)
    low = (s & half) == 0
    if 2 * half < SUBLANES:
        b = pltpu.roll(b, half, axis=0)
    return jnp.where(low, a, b) + pltpu.roll(jnp.where(low, b, a), SUBLANES - half, axis=0)


def _split_indices(eidx_t):
    n = eidx_t.shape[1]
    return eidx_t.reshape(PEER_OCTETS, SUBLANES, n).transpose(1, 2, 0).reshape(SUBLANES, n * PEER_OCTETS)


def _for_each_index_slot(idx_hbm, sm_refs, sem, tb, process):
    step = pl.program_id(0)
    words = tb * PEER_OCTETS

    def copies(block, slot):
        return [pltpu.make_async_copy(idx_hbm.at[i, pl.ds(block * words, words)], sm_refs[slot][i], sem.at[slot, i])
                for i in range(SUBLANES)]

    def start(block, slot):
        for c in copies(block, slot):
            c.start()

    @pl.when(step == 0)
    def _():
        start(0, 0)

    for slot in range(PEER_SLOTS):
        block = step * PEER_SLOTS + slot
        for c in copies(block, slot):
            c.wait()
        if slot + 1 < PEER_SLOTS:
            start(block + 1, slot + 1)
        else:
            @pl.when(step + 1 < pl.num_programs(0))
            def _():
                start(block + 1, 0)
        process(slot)


def _index_scratch(tb):
    return ([pltpu.SMEM((tb * PEER_OCTETS,), jnp.int32) for _ in range(PEER_SLOTS * SUBLANES)]
            + [pltpu.SemaphoreType.DMA((PEER_SLOTS, SUBLANES))])


def _index_refs(refs):
    return [refs[s * SUBLANES:(s + 1) * SUBLANES] for s in range(PEER_SLOTS)]


def _peer_a_kernel(idx_hbm, x_ref, gate_ref, tab_ref, w_ref, *scratch, tb):
    sm_refs, sem, q_ref, h_ref = _index_refs(scratch), scratch[-3], scratch[-2], scratch[-1]
    lane = lax.broadcasted_iota(jnp.int32, (SUBLANES, tb), 1)

    def place(sums, t):
        h = h_ref[...]
        for octet, col in enumerate(sums):
            rows = slice(octet * SUBLANES, (octet + 1) * SUBLANES)
            h_ref[rows, :] = jnp.where(lane == t, col, h[rows, :])

    def process(slot):
        t0 = slot * tb
        sm = sm_refs[slot]
        h_ref[...] = jnp.zeros_like(h_ref)
        q_ref[...] = jnp.zeros_like(q_ref)

        def lane_sums():
            q = q_ref[...]
            return [jnp.sum(q[o * SUBLANES:(o + 1) * SUBLANES, :], axis=1, keepdims=True) for o in range(PEER_OCTETS)]

        def tok(t, carry):
            sums = lane_sums()
            xv = x_ref[t0 + t]
            for octet in range(PEER_OCTETS):
                base = t * PEER_OCTETS + octet
                tiles = [_load_row(tab_ref, sm[_BITREV[i]][base]) * xv for i in range(SUBLANES)]
                half = SUBLANES // 2
                while len(tiles) > 1:
                    tiles = [_fold_sublanes(tiles[i], tiles[i + 1], half) for i in range(0, len(tiles), 2)]
                    half //= 2
                q_ref[octet * SUBLANES:(octet + 1) * SUBLANES, :] = tiles[0]
            place(sums, t - 1)
            return carry

        lax.fori_loop(0, tb, tok, 0)
        place(lane_sums(), tb - 1)
        h = h_ref[...]
        w_ref[:, t0:t0 + tb] = gate_ref[:, t0:t0 + tb] * (0.5 * h * (1.0 + lax.erf(h * math.sqrt(0.5))))

    _for_each_index_slot(idx_hbm, sm_refs, sem, tb, process)


def _peer_a(eidx_split, xn3, gate_t, tab, tb):
    n = xn3.shape[0]
    span = PEER_SLOTS * tb
    return pl.pallas_call(
        functools.partial(_peer_a_kernel, tb=tb),
        grid=(n // span,),
        in_specs=[pl.BlockSpec(memory_space=pl.ANY),
                  pl.BlockSpec((span, SUBLANES, LANES), lambda i: (i, 0, 0)),
                  pl.BlockSpec((PEER_PICKS, span), lambda i: (0, i)),
                  pl.BlockSpec(tab.shape, lambda i: (0, 0), pipeline_mode=pl.Buffered(1))],
        out_specs=pl.BlockSpec((PEER_PICKS, span), lambda i: (0, i)),
        out_shape=jax.ShapeDtypeStruct((PEER_PICKS, n), F32),
        scratch_shapes=_index_scratch(tb) + [pltpu.VMEM((PEER_PICKS, LANES), F32), pltpu.VMEM((PEER_PICKS, tb), F32)],
        compiler_params=_cparams("arbitrary"),
        name="peer_a",
    )(eidx_split, xn3, gate_t, tab)


def _peer_b_kernel(idx_hbm, wt_ref, h2_ref, gain_ref, tab_ref, y_ref, *scratch, tb):
    sm_refs, sem, wb_refs = _index_refs(scratch), scratch[-3], scratch[-2:]
    nacc = 2

    def process(slot):
        t0 = slot * tb
        sm = sm_refs[slot]
        wt = wt_ref[:, t0:t0 + tb]

        def spread(wb_ref, t):
            wb_ref[...] = jnp.take_along_axis(wt, jnp.full((PEER_PICKS, LANES), t, jnp.int32), axis=1)

        spread(wb_refs[0], 0)

        def pair(tp, carry):
            for par in range(2):
                t = 2 * tp + par
                wb_ref = wb_refs[par]
                spread(wb_refs[1 - par], jnp.minimum(t + 1, tb - 1))
                acc = [jnp.zeros((SUBLANES, LANES), F32) for _ in range(nacc)]
                for octet in range(PEER_OCTETS):
                    base = t * PEER_OCTETS + octet
                    wg = wb_ref[octet * SUBLANES:(octet + 1) * SUBLANES, :]
                    for i in range(SUBLANES):
                        acc[i % nacc] = acc[i % nacc] + wg[i:i + 1, :] * _load_row(tab_ref, sm[i][base])
                total = acc[0]
                for a in acc[1:]:
                    total = total + a
                y_ref[t0 + t] = h2_ref[t0 + t] + total
            return carry

        lax.fori_loop(0, tb // 2, pair, 0)

    _for_each_index_slot(idx_hbm, sm_refs, sem, tb, process)

    y = y_ref[...]
    ms = jnp.sum(jnp.sum(y * y, axis=2, keepdims=True), axis=1, keepdims=True) * (1.0 / D_MODEL)
    y_ref[...] = y * lax.rsqrt(ms + EPS) * gain_ref[...][None]


def _peer_b(eidx_split, w_t, h23, gain3, tab, tb):
    n = h23.shape[0]
    span = PEER_SLOTS * tb
    tok3 = pl.BlockSpec((span, SUBLANES, LANES), lambda i: (i, 0, 0))
    return pl.pallas_call(
        functools.partial(_peer_b_kernel, tb=tb),
        grid=(n // span,),
        in_specs=[pl.BlockSpec(memory_space=pl.ANY),
                  pl.BlockSpec((PEER_PICKS, span), lambda i: (0, i)),
                  tok3, pl.BlockSpec(gain3.shape, lambda i: (0, 0)),
                  pl.BlockSpec(tab.shape, lambda i: (0, 0), pipeline_mode=pl.Buffered(1))],
        out_specs=tok3,
        out_shape=jax.ShapeDtypeStruct(h23.shape, F32),
        scratch_shapes=_index_scratch(tb) + [pltpu.VMEM((PEER_PICKS, LANES), F32) for _ in range(2)],
        compiler_params=_cparams("arbitrary"),
        name="peer_b",
    )(eidx_split, w_t, h23, gain3, tab)


def _mix_and_ffn(x, oa, ob, sga, sgb, wts):
    n = x.shape[0]
    tm, tk, tb = _tile(n, 256), _tile(n, LANES), LANES
    assert n % (PEER_SLOTS * tb) == 0, "PEER kernels take whole groups of token blocks"
    h2, xn, scores_t = _merge(x, oa, ob, sga, sgb, wts["wpa"], wts["wpb"], wts["wo"], wts["gain_ffn"],
                              wts["wq"], wts["sk"], tm)
    eidx_t, gate_t = _topk(scores_t, tk)
    eidx_split = _split_indices(eidx_t * ROW_PARTS)
    w_t = _peer_a(eidx_split, xn.reshape(n, SUBLANES, LANES), gate_t, wts["u"], tb)
    y3 = _peer_b(eidx_split, w_t, h2.reshape(n, SUBLANES, LANES), wts["gain_final"], wts["v"], tb)
    return y3.reshape(n, D_MODEL)


def kernel(x_prompt, x_sample, cache_win_k, cache_win_v, cache_meta_k, cache_meta_v, state_hgrn, meta_tokens,
           norm_mix, w_in, hg_lb, hg_norm, attn_sinks, w_pa, w_pb, w_o, norm_ffn, peer_wq, peer_subkeys,
           peer_u, peer_v, norm_final):
    depth = w_in.shape[0]
    assert depth == 1, "single-layer step only"
    b, seq, _ = x_prompt.shape
    db, ds, _ = x_sample.shape
    l = 0
    lb = jnp.cumsum(jax.nn.softmax(hg_lb.astype(F32), axis=0), axis=0)[l][None, :]
    gain_mix = norm_mix[l][None, :]
    w_in_b = w_in[l].astype(BF16)
    hg_gain = hg_norm[l][None, :]
    sinks = attn_sinks[l].astype(F32)
    wts = dict(
        wpa=w_pa[l].astype(BF16), wpb=w_pb[l].astype(BF16), wo=w_o[l].astype(BF16),
        gain_ffn=norm_ffn[l][None, :], wq=peer_wq[l].astype(BF16),
        sk=peer_subkeys[l].reshape(2 * PEER_HEADS, PEER_NKEYS, PEER_HALF).astype(BF16),
        u=_table_rows(peer_u[l]), v=_table_rows(peer_v[l]),
        gain_final=norm_final.reshape(SUBLANES, LANES),
    )

    mq, mk, mlf, mv, msg, _, km, vm, _, _ = _proj(meta_tokens, gain_mix, lb, w_in_b, N_META)
    zero_state = jnp.zeros((1, HG_HEADS, HG_DK, HG_DK), F32)
    _, s_meta = _hgrn(mq, mk, mlf, mv, msg, hg_gain, zero_state, 1, N_META, N_META)

    xp = x_prompt.reshape(b * seq, D_MODEL)
    q, k, lf, v, sg, aq, ak, av, sga, sgb = _proj(xp, gain_mix, lb, w_in_b, _tile(b * seq, 256))
    oa, s_fin = _hgrn(q, k, lf, v, sg, hg_gain, s_meta, b, HG_CHUNK, _tile(seq, 512))
    ob = _swa_prompt(sinks, aq, ak, av, km, vm, b)
    y_prompt = _mix_and_ffn(xp, oa, ob, sga, sgb, wts).reshape(b, seq, D_MODEL)
    akr = ak.reshape(b, seq, AT_KV_HEADS, AT_HEAD_DIM)
    avr = av.reshape(b, seq, AT_KV_HEADS, AT_HEAD_DIM)
    kmr = jnp.broadcast_to(km.reshape(1, N_META, AT_KV_HEADS, AT_HEAD_DIM), (b, N_META, AT_KV_HEADS, AT_HEAD_DIM))
    vmr = jnp.broadcast_to(vm.reshape(1, N_META, AT_KV_HEADS, AT_HEAD_DIM), (b, N_META, AT_KV_HEADS, AT_HEAD_DIM))

    xs = x_sample.reshape(db * ds, D_MODEL)
    q, k, lf, v, sg, aq, aks, avs, sga, sgb = _proj(xs, gain_mix, lb, w_in_b, _tile(db * ds, 256))
    cs = SUBLANES
    pad = lambda a: jnp.pad(a.reshape(db, ds, HG_WIDTH), ((0, 0), (0, cs - ds), (0, 0))).reshape(db * cs, HG_WIDTH)
    oa, s_new = _hgrn(pad(q), pad(k), pad(lf), pad(v), pad(sg), hg_gain, state_hgrn[l].astype(F32), db, cs, cs)
    oa = oa.reshape(db, cs, HG_WIDTH)[:, :ds].reshape(db * ds, HG_WIDTH)
    ob = _swa_sample(sinks, aq.reshape(db, ds, AT_WIDTH), aks.reshape(db, ds, AT_KV_WIDTH),
                     avs.reshape(db, ds, AT_KV_WIDTH),
                     cache_win_k[l].reshape(db, WINDOW, AT_KV_WIDTH), cache_win_v[l].reshape(db, WINDOW, AT_KV_WIDTH),
                     cache_meta_k[l].reshape(db, N_META, AT_KV_WIDTH), cache_meta_v[l].reshape(db, N_META, AT_KV_WIDTH),
                     8).reshape(db * ds, AT_WIDTH)
    y_sample = _mix_and_ffn(xs, oa, ob, sga, sgb, wts).reshape(db, ds, D_MODEL)

    return (y_prompt, y_sample,
            akr[:, -WINDOW:][None], avr[:, -WINDOW:][None], kmr[None], vmr[None], s_fin[None],
            aks.reshape(1, db, ds, AT_KV_HEADS, AT_HEAD_DIM), avs.reshape(1, db, ds, AT_KV_HEADS, AT_HEAD_DIM),
            s_new[None])
```
